```python
import jax, jax.numpy as jnp
from jax import lax
import numpy as np

D_MODEL = 2048
BATCH = 2
SEQ = 16384
DEPTH = 1

GRID_W = 64
HEAD_DIM = 128
A_Q_HEADS = 8
A_KV_HEADS = 2
B_HEADS = 8
NA_ROWS = 8
NA_COLS = 16
ROPE_THETA = 10000.0
Q_BLOCK = 128
N_EXPERTS = 32
TOP_K = 4
D_EXPERT = D_MODEL
SWIGLU_ALPHA = 1.702
SWIGLU_LIMIT = 7.0
MOE_BLOCK = 256
EPS = 1e-6

A_Q_W = A_Q_HEADS * HEAD_DIM
A_KV_W = A_KV_HEADS * HEAD_DIM
B_W = B_HEADS * HEAD_DIM
IN_SIZES = (A_Q_W, A_KV_W, A_KV_W, B_W, B_W, B_W, D_MODEL, D_MODEL)
IN_COLS = sum(IN_SIZES)

kernel_name = "hybrid_gqa_axialrope_natten_moe_encoder"


def rmsnorm(x, g):
    x32 = x.astype(jnp.float32)
    y = x32 * lax.rsqrt(jnp.mean(x32 * x32, axis=-1, keepdims=True) + EPS)
    return y.astype(x.dtype) * g


def axial_rope_tables(seq_len):
    t = jnp.arange(seq_len, dtype=jnp.int32)
    pos = jnp.stack([(t // GRID_W), (t % GRID_W)], axis=-1).astype(jnp.float32)
    axis_dims = HEAD_DIM // 2
    inv_freq = ROPE_THETA ** (-jnp.arange(0, axis_dims, 2, dtype=jnp.float32) / axis_dims)
    ang = pos[:, :, None] * inv_freq[None, None, :]
    return jnp.cos(ang), jnp.sin(ang)


def apply_axial_rope(x, cos, sin):
    B, S, H, dh = x.shape
    xr = x.astype(jnp.float32).reshape(B, S, H, 2, 2, dh // 4)
    x1, x2 = xr[..., 0, :], xr[..., 1, :]
    c, s = cos[None, :, None], sin[None, :, None]
    out = jnp.stack([x1 * c - x2 * s, x2 * c + x1 * s], axis=-2)
    return out.reshape(B, S, H, dh).astype(x.dtype)


def gqa_axial_attention(q, k, v, cos, sin):
    B, S, _, dh = q.shape
    group = A_Q_HEADS // A_KV_HEADS
    q = apply_axial_rope(q, cos, sin)
    k = apply_axial_rope(k, cos, sin)
    scale = dh ** -0.5
    qb = jnp.moveaxis(q.reshape(B, S // Q_BLOCK, Q_BLOCK, A_KV_HEADS, group, dh), 1, 0)

    def block(q_blk):
        s = jnp.einsum('bqhgd,bkhd->bhgqk', q_blk, k, preferred_element_type=jnp.float32) * scale
        p = jax.nn.softmax(s, axis=-1).astype(v.dtype)
        return jnp.einsum('bhgqk,bkhd->bqhgd', p, v)

    o = lax.map(block, qb)
    return jnp.moveaxis(o, 0, 1).reshape(B, S, A_Q_HEADS * dh)


def neighbourhood_attention(q, k, v, rel_bias):
    B, S, H, dh = q.shape
    rows = S // GRID_W
    kr = min(NA_ROWS, rows)
    kc = NA_COLS
    scale = dh ** -0.5
    qg = q.reshape(B, rows, GRID_W, H, dh)
    kg = k.reshape(B, rows, GRID_W, H, dh)
    vg = v.reshape(B, rows, GRID_W, H, dh)
    c = np.arange(GRID_W)
    c_start = np.clip(c - kc // 2, 0, GRID_W - kc)
    col_idx = c_start[:, None] + np.arange(kc)[None, :]
    col_off = col_idx - c[:, None] + (NA_COLS - 1)
    bias_cols = rel_bias[:, :, col_off]

    def row_block(r):
        r_start = jnp.clip(r - kr // 2, 0, rows - kr)
        q_r = lax.dynamic_index_in_dim(qg, r, axis=1, keepdims=False)
        k_win = lax.dynamic_slice_in_dim(kg, r_start, kr, axis=1)
        v_win = lax.dynamic_slice_in_dim(vg, r_start, kr, axis=1)
        k_sel = k_win[:, :, col_idx]
        v_sel = v_win[:, :, col_idx]
        s = jnp.einsum('bchd,bicjhd->bhcij', q_r, k_sel, preferred_element_type=jnp.float32) * scale
        row_off = r_start + jnp.arange(kr) - r + (NA_ROWS - 1)
        bias = jnp.take(bias_cols, row_off, axis=1)
        s = s + jnp.transpose(bias, (0, 2, 1, 3))[None].astype(jnp.float32)
        p = jax.nn.softmax(s.reshape(B, H, GRID_W, kr * kc), axis=-1)
        p = p.reshape(B, H, GRID_W, kr, kc).astype(v.dtype)
        return jnp.einsum('bhcij,bicjhd->bchd', p, v_sel)

    o = lax.map(row_block, jnp.arange(rows, dtype=jnp.int32))
    return jnp.moveaxis(o, 0, 1).reshape(B, S, H * dh)


def clamped_swiglu(gate, up):
    gate = jnp.minimum(gate, SWIGLU_LIMIT)
    up = jnp.clip(up, -SWIGLU_LIMIT, SWIGLU_LIMIT)
    return gate * jax.nn.sigmoid(SWIGLU_ALPHA * gate) * (up + 1.0)


def moe_ffn(h, w_router, b_router, w_gate, b_gate, w_up, b_up, w_down, b_down):
    n, d = h.shape
    logits = jnp.dot(h, w_router, preferred_element_type=jnp.float32) + b_router.astype(jnp.float32)
    top_val, top_idx = lax.top_k(logits, TOP_K)
    top_w = jax.nn.softmax(top_val, axis=-1)
    nk = n * TOP_K
    flat_e = top_idx.reshape(nk)
    flat_tok = jnp.arange(nk, dtype=jnp.int32) // TOP_K
    order = jnp.argsort(flat_e)
    sorted_e = flat_e[order]
    counts = jnp.bincount(flat_e, length=N_EXPERTS)
    padded = (counts + MOE_BLOCK - 1) // MOE_BLOCK * MOE_BLOCK
    pad_end = jnp.cumsum(padded)
    pad_start = pad_end - padded
    start = jnp.cumsum(counts) - counts
    rank = jnp.arange(nk, dtype=jnp.int32) - start[sorted_e]
    dest = pad_start[sorted_e] + rank
    n_blocks = -(-nk // MOE_BLOCK) + N_EXPERTS
    n_slots = n_blocks * MOE_BLOCK
    slot_tok = jnp.full((n_slots,), n, jnp.int32).at[dest].set(flat_tok[order])
    slot_w = jnp.zeros((n_slots,), h.dtype).at[dest].set(top_w.reshape(nk)[order].astype(h.dtype))
    block_start = jnp.arange(n_blocks, dtype=jnp.int32) * MOE_BLOCK
    block_e = jnp.minimum(jnp.sum(block_start[:, None] >= pad_end[None, :], axis=1), N_EXPERTS - 1)
    h_pad = jnp.concatenate([h, jnp.zeros((1, d), h.dtype)], axis=0)
    xb = h_pad[slot_tok].reshape(n_blocks, MOE_BLOCK, d)

    def expert_block(args):
        xe, e = args
        gate = xe @ w_gate[e] + b_gate[e]
        up = xe @ w_up[e] + b_up[e]
        return clamped_swiglu(gate, up) @ w_down[e] + b_down[e]

    yb = lax.map(expert_block, (xb, block_e)).reshape(n_slots, d)
    out = jax.ops.segment_sum(yb * slot_w[:, None], slot_tok, num_segments=n + 1)
    return out[:n]


def setup_inputs(seed: int = 0) -> dict:
    key = jax.random.key(seed)
    ks = jax.random.split(key, 24)
    f32 = jnp.float32

    def nrm(k, shape, scale):
        return jax.random.normal(k, shape, f32) * scale

    def gain(k, shape):
        return 1.0 + 0.02 * jax.random.normal(k, shape, f32)

    return {
        "x": jax.random.normal(ks[0], (BATCH, SEQ, D_MODEL), f32),
        "attn_norm_g": gain(ks[1], (DEPTH, D_MODEL)),
        "w_in": nrm(ks[2], (DEPTH, D_MODEL, IN_COLS), D_MODEL ** -0.5),
        "a_q_norm_g": gain(ks[3], (DEPTH, HEAD_DIM)),
        "a_k_norm_g": gain(ks[4], (DEPTH, HEAD_DIM)),
        "b_q_norm_g": gain(ks[5], (DEPTH, HEAD_DIM)),
        "b_k_norm_g": gain(ks[6], (DEPTH, HEAD_DIM)),
        "na_rel_bias": nrm(ks[7], (DEPTH, B_HEADS, 2 * NA_ROWS - 1, 2 * NA_COLS - 1), 0.1),
        "w_branch_a": nrm(ks[8], (DEPTH, A_Q_W, D_MODEL), A_Q_W ** -0.5),
        "w_branch_b": nrm(ks[9], (DEPTH, B_W, D_MODEL), B_W ** -0.5),
        "w_out": nrm(ks[10], (DEPTH, D_MODEL, D_MODEL), D_MODEL ** -0.5),
        "ffn_norm_g": gain(ks[11], (DEPTH, D_MODEL)),
        "w_router": nrm(ks[12], (DEPTH, D_MODEL, N_EXPERTS), D_MODEL ** -0.5),
        "b_router": nrm(ks[13], (DEPTH, N_EXPERTS), 0.01),
        "w_gate": nrm(ks[14], (DEPTH, N_EXPERTS, D_MODEL, D_EXPERT), D_MODEL ** -0.5),
        "b_gate": nrm(ks[15], (DEPTH, N_EXPERTS, D_EXPERT), 0.01),
        "w_up": nrm(ks[16], (DEPTH, N_EXPERTS, D_MODEL, D_EXPERT), D_MODEL ** -0.5),
        "b_up": nrm(ks[17], (DEPTH, N_EXPERTS, D_EXPERT), 0.01),
        "w_down": nrm(ks[18], (DEPTH, N_EXPERTS, D_EXPERT, D_MODEL), D_EXPERT ** -0.5),
        "b_down": nrm(ks[19], (DEPTH, N_EXPERTS, D_MODEL), 0.01),
    }


def reference(x, attn_norm_g, w_in, a_q_norm_g, a_k_norm_g, b_q_norm_g, b_k_norm_g,
              na_rel_bias, w_branch_a, w_branch_b, w_out, ffn_norm_g, w_router, b_router,
              w_gate, b_gate, w_up, b_up, w_down, b_down):
    B, S, D = x.shape
    cos, sin = axial_rope_tables(S)
    split_at = [int(v) for v in np.cumsum(IN_SIZES)[:-1]]
    for l in range(DEPTH):
        h = rmsnorm(x, attn_norm_g[l])
        proj = h @ w_in[l]
        qa, ka, va, qb, kb, vb, ga, gb = jnp.split(proj, split_at, axis=-1)
        qa = rmsnorm(qa.reshape(B, S, A_Q_HEADS, HEAD_DIM), a_q_norm_g[l])
        ka = rmsnorm(ka.reshape(B, S, A_KV_HEADS, HEAD_DIM), a_k_norm_g[l])
        va = va.reshape(B, S, A_KV_HEADS, HEAD_DIM)
        qb = rmsnorm(qb.reshape(B, S, B_HEADS, HEAD_DIM), b_q_norm_g[l])
        kb = rmsnorm(kb.reshape(B, S, B_HEADS, HEAD_DIM), b_k_norm_g[l])
        vb = vb.reshape(B, S, B_HEADS, HEAD_DIM)
        ya = gqa_axial_attention(qa, ka, va, cos, sin)
        yb = neighbourhood_attention(qb, kb, vb, na_rel_bias[l])
        merged = jax.nn.sigmoid(ga) * (ya @ w_branch_a[l]) + jax.nn.sigmoid(gb) * (yb @ w_branch_b[l])
        x = x + merged @ w_out[l]
        h = rmsnorm(x, ffn_norm_g[l]).reshape(B * S, D)
        y = moe_ffn(h, w_router[l], b_router[l], w_gate[l], b_gate[l], w_up[l], b_up[l],
                    w_down[l], b_down[l])
        x = x + y.reshape(B, S, D)
    return x
```

```python
import functools

import numpy as np
import jax
import jax.numpy as jnp
from jax import lax
from jax.experimental import pallas as pl
from jax.experimental.pallas import tpu as pltpu

GRID_W = 64
HEAD_DIM = 128
A_Q_HEADS = 8
A_KV_HEADS = 2
B_HEADS = 8
NA_ROWS = 8
NA_COLS = 16
ROPE_THETA = 10000.0
TOP_K = 4
SWIGLU_ALPHA = 1.702
SWIGLU_LIMIT = 7.0
EPS = 1e-6

A_Q_W = A_Q_HEADS * HEAD_DIM
A_KV_W = A_KV_HEADS * HEAD_DIM
B_W = B_HEADS * HEAD_DIM
GROUP = A_Q_HEADS // A_KV_HEADS

LANES = 128
NEG = -1e30
NA_CHUNK_ROWS = 8
NA_KEY_ROWS = 2 * NA_CHUNK_ROWS
MOE_TM = 512
VMEM_LIMIT = 56 * 1024 * 1024

F32 = jnp.float32
BF16 = jnp.bfloat16
U32 = jnp.uint32
HI_MASK = np.uint32(0xFFFF0000)


def _tile(n, pref, mult):
    if n <= pref:
        return n
    t = (pref // mult) * mult
    while t >= mult:
        if n % t == 0:
            return t
        t -= mult
    raise ValueError(f"no tile for {n} (pref {pref}, multiple of {mult})")


def _cparams(sem):
    return pltpu.CompilerParams(dimension_semantics=sem, vmem_limit_bytes=VMEM_LIMIT)


def _pack_bf16_pairs(v):
    h = v.shape[1] // 2
    bits = lax.bitcast_convert_type(v.astype(BF16).astype(F32), U32)
    return (bits[:, :h] >> 16) | (bits[:, h:] & HI_MASK)


def _unpack_lo(u):
    return lax.bitcast_convert_type(u << 16, F32)


def _unpack_hi(u):
    return lax.bitcast_convert_type(u & HI_MASK, F32)


def _inproj_kernel(x_ref, g_ref, w_ref, o_ref, xn_ref):
    @pl.when(pl.program_id(1) == 0)
    def _():
        x = x_ref[...]
        ms = jnp.mean(x * x, axis=-1, keepdims=True)
        xn_ref[...] = (x * lax.rsqrt(ms + EPS) * g_ref[...]).astype(BF16)

    o_ref[...] = jnp.dot(xn_ref[...], w_ref[...], preferred_element_type=F32).astype(o_ref.dtype)


def _inproj(x2, g, w):
    n, d = x2.shape
    wcols = w.shape[1]
    tm = _tile(n, 1024, 8)
    tn = _tile(wcols, 512, LANES)
    return pl.pallas_call(
        _inproj_kernel,
        grid=(n // tm, wcols // tn),
        in_specs=[
            pl.BlockSpec((tm, d), lambda i, j: (i, 0)),
            pl.BlockSpec((1, d), lambda i, j: (0, 0)),
            pl.BlockSpec((d, tn), lambda i, j: (0, j)),
        ],
        out_specs=pl.BlockSpec((tm, tn), lambda i, j: (i, j)),
        out_shape=jax.ShapeDtypeStruct((n, wcols), BF16),
        scratch_shapes=[pltpu.VMEM((tm, d), BF16)],
        compiler_params=_cparams(("parallel", "arbitrary")),
        name="inproj",
    )(x2, g, w)


def _prep_kernel(qa_ref, ka_ref, va_ref, qb_ref, kb_ref, cos_ref, sin_ref,
                 gqa_ref, gka_ref, gqb_ref, gkb_ref,
                 qa_o, ka_o, vat_o, qb_o, kb_o):
    scale = HEAD_DIM ** -0.5
    cos = cos_ref[...]
    sin = sin_ref[...]
    lane = lax.broadcasted_iota(jnp.int32, (1, HEAD_DIM), 1)
    first_half = (lane % (HEAD_DIM // 2)) < (HEAD_DIM // 4)

    def norm(xh, g):
        ms = jnp.mean(xh * xh, axis=-1, keepdims=True)
        return xh * lax.rsqrt(ms + EPS) * g

    def rope(xh):
        quarter = HEAD_DIM // 4
        partner = jnp.where(first_half, pltpu.roll(xh, HEAD_DIM - quarter, 1), pltpu.roll(xh, quarter, 1))
        return xh * cos + partner * sin

    def head(ref, h):
        return ref[:, h * HEAD_DIM:(h + 1) * HEAD_DIM].astype(F32)

    for h in range(A_Q_HEADS):
        qa_o[:, h * HEAD_DIM:(h + 1) * HEAD_DIM] = (rope(norm(head(qa_ref, h), gqa_ref[...])) * scale).astype(BF16)
    for h in range(A_KV_HEADS):
        ka_o[:, h * HEAD_DIM:(h + 1) * HEAD_DIM] = rope(norm(head(ka_ref, h), gka_ref[...])).astype(BF16)
        vat_o[0, h] = head(va_ref, h).T.astype(BF16)
    for h in range(B_HEADS):
        qb_o[:, h * HEAD_DIM:(h + 1) * HEAD_DIM] = (norm(head(qb_ref, h), gqb_ref[...]) * scale).astype(BF16)
        kb_o[:, h * HEAD_DIM:(h + 1) * HEAD_DIM] = norm(head(kb_ref, h), gkb_ref[...]).astype(BF16)


def _prep(proj, cos_t, sin_t, gqa, gka, gqb, gkb, b, s, off):
    n = proj.shape[0]
    tm = _tile(s, 512, LANES)
    spb = s // tm

    def col(width, offset):
        assert offset % width == 0
        return pl.BlockSpec((tm, width), lambda i, o=offset // width: (i, o))

    gspec = pl.BlockSpec((1, HEAD_DIM), lambda i: (0, 0))
    tspec = pl.BlockSpec((tm, HEAD_DIM), lambda i: (i % spb, 0))
    return pl.pallas_call(
        _prep_kernel,
        grid=(n // tm,),
        in_specs=[col(A_Q_W, off["qa"]), col(A_KV_W, off["ka"]), col(A_KV_W, off["va"]),
                  col(B_W, off["qb"]), col(B_W, off["kb"]), tspec, tspec, gspec, gspec, gspec, gspec],
        out_specs=[
            pl.BlockSpec((tm, A_Q_W), lambda i: (i, 0)),
            pl.BlockSpec((tm, A_KV_W), lambda i: (i, 0)),
            pl.BlockSpec((1, A_KV_HEADS, HEAD_DIM, tm), lambda i: (i // spb, 0, 0, i % spb)),
            pl.BlockSpec((tm, B_W), lambda i: (i, 0)),
            pl.BlockSpec((tm, B_W), lambda i: (i, 0)),
        ],
        out_shape=[
            jax.ShapeDtypeStruct((n, A_Q_W), BF16),
            jax.ShapeDtypeStruct((n, A_KV_W), BF16),
            jax.ShapeDtypeStruct((b, A_KV_HEADS, HEAD_DIM, s), BF16),
            jax.ShapeDtypeStruct((n, B_W), BF16),
            jax.ShapeDtypeStruct((n, B_W), BF16),
        ],
        compiler_params=_cparams(("parallel",)),
        name="qk_prep",
    )(proj, proj, proj, proj, proj, cos_t, sin_t, gqa, gka, gqb, gkb)


def _flash_kernel(q_ref, k_ref, vt_ref, o_ref, m_ref, l_ref, acc_ref):
    ki = pl.program_id(3)

    @pl.when(ki == 0)
    def _():
        m_ref[...] = jnp.full(m_ref.shape, NEG, F32)
        l_ref[...] = jnp.zeros(l_ref.shape, F32)
        acc_ref[...] = jnp.zeros(acc_ref.shape, F32)

    k = k_ref[0]
    vt = vt_ref[0, 0]
    for g in range(GROUP):
        q = q_ref[0, :, g * HEAD_DIM:(g + 1) * HEAD_DIM]
        st = lax.dot_general(k, q, (((1,), (1,)), ((), ())), preferred_element_type=F32)
        m_prev = m_ref[g:g + 1, :]
        m_new = jnp.maximum(m_prev, jnp.max(st, axis=0, keepdims=True))
        alpha = jnp.exp(m_prev - m_new)
        p = jnp.exp(st - m_new)
        l_ref[g:g + 1, :] = alpha * l_ref[g:g + 1, :] + jnp.sum(p, axis=0, keepdims=True)
        acc_ref[g] = acc_ref[g] * alpha + jnp.dot(vt, p.astype(BF16), preferred_element_type=F32)
        m_ref[g:g + 1, :] = m_new

    @pl.when(ki == pl.num_programs(3) - 1)
    def _():
        for g in range(GROUP):
            o = acc_ref[g] / l_ref[g:g + 1, :]
            o_ref[0, :, g * HEAD_DIM:(g + 1) * HEAD_DIM] = o.T.astype(o_ref.dtype)


def _flash(qa, ka, vat):
    b, s, _ = qa.shape
    tq = _tile(s, 512, LANES)
    tk = _tile(s, 1024, LANES)
    gw = GROUP * HEAD_DIM
    return pl.pallas_call(
        _flash_kernel,
        grid=(b, A_KV_HEADS, s // tq, s // tk),
        in_specs=[
            pl.BlockSpec((1, tq, gw), lambda bi, h, qi, ki: (bi, qi, h)),
            pl.BlockSpec((1, tk, HEAD_DIM), lambda bi, h, qi, ki: (bi, ki, h)),
            pl.BlockSpec((1, 1, HEAD_DIM, tk), lambda bi, h, qi, ki: (bi, h, 0, ki)),
        ],
        out_specs=pl.BlockSpec((1, tq, gw), lambda bi, h, qi, ki: (bi, qi, h)),
        out_shape=jax.ShapeDtypeStruct((b, s, A_Q_W), BF16),
        scratch_shapes=[pltpu.VMEM((8, tq), F32), pltpu.VMEM((8, tq), F32),
                        pltpu.VMEM((GROUP, HEAD_DIM, tq), F32)],
        compiler_params=_cparams(("parallel", "parallel", "parallel", "arbitrary")),
        name="gqa_flash",
    )(qa, ka, vat)


def _na_kernel(q_ref, kp_ref, kc_ref, kn_ref, vp_ref, vc_ref, vn_ref, bias_ref, o_ref):
    half = (NA_CHUNK_ROWS // 2) * GRID_W
    q = q_ref[0]
    k = jnp.concatenate([kp_ref[0, half:, :], kc_ref[0], kn_ref[0, :half, :]], axis=0)
    v = jnp.concatenate([vp_ref[0, half:, :], vc_ref[0], vn_ref[0, :half, :]], axis=0)
    s = lax.dot_general(q, k, (((1,), (1,)), ((), ())), preferred_element_type=F32) + bias_ref[0, 0]
    m = jnp.max(s, axis=-1, keepdims=True)
    p = jnp.exp(s - m)
    l = jnp.sum(p, axis=-1, keepdims=True)
    o = jnp.dot(p.astype(BF16), v, preferred_element_type=F32) / l
    o_ref[0] = o.astype(o_ref.dtype)


def _na_bias_table(rel_bias, rows):
    n_chunks = rows // NA_CHUNK_ROWS
    ql = np.arange(NA_CHUNK_ROWS * GRID_W)
    kl = np.arange(NA_KEY_ROWS * GRID_W)
    qr_l, qc = ql // GRID_W, ql % GRID_W
    kr_l, kc = kl // GRID_W, kl % GRID_W
    c_start = np.clip(qc - NA_COLS // 2, 0, GRID_W - NA_COLS)
    valid_c = (kc[None, :] >= c_start[:, None]) & (kc[None, :] < c_start[:, None] + NA_COLS)
    col_off = np.clip(kc[None, :] - qc[:, None] + (NA_COLS - 1), 0, 2 * NA_COLS - 2)
    tables = []
    for c in (0, min(1, n_chunks - 1), n_chunks - 1):
        qr = NA_CHUNK_ROWS * c + qr_l
        kr = NA_CHUNK_ROWS * c - NA_CHUNK_ROWS // 2 + kr_l
        r_start = np.clip(qr - NA_ROWS // 2, 0, rows - NA_ROWS)
        valid_r = (kr[None, :] >= r_start[:, None]) & (kr[None, :] < r_start[:, None] + NA_ROWS)
        row_off = np.clip(kr[None, :] - qr[:, None] + (NA_ROWS - 1), 0, 2 * NA_ROWS - 2)
        vals = rel_bias[:, row_off, col_off].astype(F32)
        tables.append(jnp.where((valid_r & valid_c)[None], vals, NEG))
    return jnp.stack(tables, axis=0)


def _na(qb, kb, proj3, vb_off, bias_tab):
    b, s, _ = qb.shape
    tq = NA_CHUNK_ROWS * GRID_W
    n_chunks = s // tq
    vblk = vb_off // HEAD_DIM

    def variant(c):
        return jnp.where(c == 0, 0, jnp.where(c == n_chunks - 1, 2, 1))

    def kspec(shift, base):
        return pl.BlockSpec(
            (1, tq, HEAD_DIM),
            lambda bi, h, c: (bi, jnp.clip(c + shift, 0, n_chunks - 1), base + h))

    return pl.pallas_call(
        _na_kernel,
        grid=(b, B_HEADS, n_chunks),
        in_specs=[
            pl.BlockSpec((1, tq, HEAD_DIM), lambda bi, h, c: (bi, c, h)),
            kspec(-1, 0), kspec(0, 0), kspec(1, 0),
            kspec(-1, vblk), kspec(0, vblk), kspec(1, vblk),
            pl.BlockSpec((1, 1, tq, NA_KEY_ROWS * GRID_W), lambda bi, h, c: (variant(c), h, 0, 0)),
        ],
        out_specs=pl.BlockSpec((1, tq, HEAD_DIM), lambda bi, h, c: (bi, c, h)),
        out_shape=jax.ShapeDtypeStruct((b, s, B_W), BF16),
        compiler_params=_cparams(("parallel", "parallel", "arbitrary")),
        name="nbr_attn",
    )(qb, kb, kb, kb, proj3, proj3, proj3, bias_tab)


def _merge_kernel(ya_ref, yb_ref, ga_ref, gb_ref, x_ref, wpa_ref, wpb_ref, wout_ref, g2_ref, wr_ref, br_ref,
                  xmid_o, hu_o, logit_o):
    a = jnp.dot(ya_ref[...], wpa_ref[...], preferred_element_type=F32)
    bb = jnp.dot(yb_ref[...], wpb_ref[...], preferred_element_type=F32)
    m = jax.nn.sigmoid(ga_ref[...].astype(F32)) * a + jax.nn.sigmoid(gb_ref[...].astype(F32)) * bb
    xm = x_ref[...] + jnp.dot(m.astype(BF16), wout_ref[...], preferred_element_type=F32)
    xmid_o[...] = xm
    ms = jnp.mean(xm * xm, axis=-1, keepdims=True)
    h = xm * lax.rsqrt(ms + EPS) * g2_ref[...]
    logit_o[...] = jnp.dot(h, wr_ref[...], preferred_element_type=F32,
                           precision=lax.Precision.HIGHEST) + br_ref[...]
    hu_o[...] = _pack_bf16_pairs(h)


def _merge(ya, yb, proj, x2, wpa, wpb, wout, g2, wr, br, off):
    n, d = x2.shape
    tm = _tile(n, 256, 8)
    const = lambda i: (0, 0)
    wspec = lambda shape: pl.BlockSpec(shape, const, pipeline_mode=pl.Buffered(1))
    return pl.pallas_call(
        _merge_kernel,
        grid=(n // tm,),
        in_specs=[
            pl.BlockSpec((tm, A_Q_W), lambda i: (i, 0)),
            pl.BlockSpec((tm, B_W), lambda i: (i, 0)),
            pl.BlockSpec((tm, d), lambda i, o=off["ga"] // d: (i, o)),
            pl.BlockSpec((tm, d), lambda i, o=off["gb"] // d: (i, o)),
            pl.BlockSpec((tm, d), lambda i: (i, 0)),
            wspec((A_Q_W, d)), wspec((B_W, d)), wspec((d, d)),
            pl.BlockSpec((1, d), const),
            wspec((d, LANES)),
            pl.BlockSpec((1, LANES), const),
        ],
        out_specs=[
            pl.BlockSpec((tm, d), lambda i: (i, 0)),
            pl.BlockSpec((tm, d // 2), lambda i: (i, 0)),
            pl.BlockSpec((tm, LANES), lambda i: (i, 0)),
        ],
        out_shape=[
            jax.ShapeDtypeStruct((n, d), F32),
            jax.ShapeDtypeStruct((n, d // 2), U32),
            jax.ShapeDtypeStruct((n, LANES), F32),
        ],
        compiler_params=_cparams(("parallel",)),
        name="merge_outproj",
    )(ya, yb, proj, proj, x2, wpa, wpb, wout, g2, wr, br)


def _router_kernel(logit_ref, idx_o, w_o, rank_o, cnt_o, carry_ref):
    tm = logit_ref.shape[0]

    @pl.when(pl.program_id(0) == 0)
    def _():
        carry_ref[...] = jnp.zeros(carry_ref.shape, F32)

    l = logit_ref[...]
    lane = lax.broadcasted_iota(jnp.int32, (tm, LANES), 1)
    lane_f = lane.astype(F32)
    vals, idxs, hots = [], [], []
    for _ in range(TOP_K):
        mx = jnp.max(l, axis=-1, keepdims=True)
        idx = jnp.min(jnp.where(l == mx, lane_f, float(LANES)), axis=-1, keepdims=True)
        hot = lane_f == idx
        vals.append(mx)
        idxs.append(idx)
        hots.append(hot)
        l = jnp.where(hot, -3e38, l)

    exps = [jnp.exp(v - vals[0]) for v in vals]
    tot = exps[0]
    for e in exps[1:]:
        tot = tot + e

    sel = jnp.zeros((tm, LANES), F32)
    for hot in hots:
        sel = sel + hot.astype(F32)
    row = lax.broadcasted_iota(jnp.int32, (tm, tm), 0)
    colm = lax.broadcasted_iota(jnp.int32, (tm, tm), 1)
    lower = (colm < row).astype(BF16)
    before = jnp.dot(lower, sel.astype(BF16), preferred_element_type=F32) + carry_ref[...]

    idx_out = jnp.zeros((tm, LANES), F32)
    w_out = jnp.zeros((tm, LANES), F32)
    rank_out = jnp.zeros((tm, LANES), F32)
    for k in range(TOP_K):
        here = lane == k
        rank_k = jnp.sum(jnp.where(hots[k], before, 0.0), axis=-1, keepdims=True)
        idx_out = jnp.where(here, idxs[k], idx_out)
        w_out = jnp.where(here, exps[k] / tot, w_out)
        rank_out = jnp.where(here, rank_k, rank_out)
    idx_o[...] = idx_out.astype(jnp.int32)
    w_o[...] = w_out
    rank_o[...] = rank_out.astype(jnp.int32)
    carry_ref[...] = carry_ref[...] + jnp.sum(sel, axis=0, keepdims=True)
    cnt_o[...] = carry_ref[...].astype(jnp.int32)


def _router(logits):
    n = logits.shape[0]
    tm = _tile(n, 512, 8)
    blk = pl.BlockSpec((tm, LANES), lambda i: (i, 0))
    return pl.pallas_call(
        _router_kernel,
        grid=(n // tm,),
        in_specs=[blk],
        out_specs=[blk, blk, blk, pl.BlockSpec((1, LANES), lambda i: (0, 0))],
        out_shape=[
            jax.ShapeDtypeStruct((n, LANES), jnp.int32),
            jax.ShapeDtypeStruct((n, LANES), F32),
            jax.ShapeDtypeStruct((n, LANES), jnp.int32),
            jax.ShapeDtypeStruct((1, LANES), jnp.int32),
        ],
        scratch_shapes=[pltpu.VMEM((1, LANES), F32)],
        compiler_params=_cparams(("arbitrary",)),
        name="router_topk",
    )(logits)


def _dispatch_kernel(dest_ref, h_ref, xs_in, xs_out, sem):
    del xs_in
    n = dest_ref.shape[0]

    def row_copy(i):
        t = lax.div(i, TOP_K)
        return pltpu.make_async_copy(h_ref.at[pl.ds(t, 1)], xs_out.at[pl.ds(dest_ref[i], 1)], sem)

    def start(i, c):
        row_copy(i).start()
        return c

    def wait(i, c):
        row_copy(i).wait()
        return c

    lax.fori_loop(0, n, start, 0)
    lax.fori_loop(0, n, wait, 0)


def _dispatch(dest_flat, hu, n_slots):
    n, half = hu.shape
    tb = _tile(n, 256, 8)
    xs0 = jnp.zeros((n_slots, half), U32)
    return pl.pallas_call(
        _dispatch_kernel,
        grid=(n // tb,),
        in_specs=[
            pl.BlockSpec((tb * TOP_K,), lambda i: (i,), memory_space=pltpu.SMEM),
            pl.BlockSpec((tb, half), lambda i: (i, 0)),
            pl.BlockSpec(memory_space=pl.ANY),
        ],
        out_specs=pl.BlockSpec(memory_space=pl.ANY),
        out_shape=jax.ShapeDtypeStruct((n_slots, half), U32),
        scratch_shapes=[pltpu.SemaphoreType.DMA],
        input_output_aliases={2: 0},
        compiler_params=_cparams(("arbitrary",)),
        name="moe_dispatch",
    )(dest_flat, hu, xs0)


def _moe_up_kernel(be_ref, nu_ref, xs_ref, wg_ref, wu_ref, bg_ref, bu_ref, o_ref):
    del be_ref

    @pl.when(pl.program_id(1) < nu_ref[0])
    def _():
        xu = xs_ref[...]
        half = xu.shape[1]
        lo = _unpack_lo(xu).astype(BF16)
        hi = _unpack_hi(xu).astype(BF16)

        def proj(w_ref, b_ref):
            return (jnp.dot(lo, w_ref[0, :half, :], preferred_element_type=F32)
                    + jnp.dot(hi, w_ref[0, half:, :], preferred_element_type=F32) + b_ref[0])

        gate = jnp.minimum(proj(wg_ref, bg_ref), SWIGLU_LIMIT)
        up = jnp.clip(proj(wu_ref, bu_ref), -SWIGLU_LIMIT, SWIGLU_LIMIT)
        o_ref[...] = (gate * jax.nn.sigmoid(SWIGLU_ALPHA * gate) * (up + 1.0)).astype(o_ref.dtype)


def _moe_up(block_e, n_used, xs, wg, wu, bg, bu):
    n_slots, half = xs.shape
    e, d, de = wg.shape
    tn = _tile(de, 1024, LANES)
    nb = n_slots // MOE_TM

    def blk(b, nu):
        return jnp.minimum(b, nu[0] - 1)

    grid_spec = pltpu.PrefetchScalarGridSpec(
        num_scalar_prefetch=2,
        grid=(de // tn, nb),
        in_specs=[
            pl.BlockSpec((MOE_TM, half), lambda j, b, be, nu: (blk(b, nu), 0)),
            pl.BlockSpec((1, d, tn), lambda j, b, be, nu: (be[blk(b, nu)], 0, j)),
            pl.BlockSpec((1, d, tn), lambda j, b, be, nu: (be[blk(b, nu)], 0, j)),
            pl.BlockSpec((1, 1, tn), lambda j, b, be, nu: (be[blk(b, nu)], 0, j)),
            pl.BlockSpec((1, 1, tn), lambda j, b, be, nu: (be[blk(b, nu)], 0, j)),
        ],
        out_specs=pl.BlockSpec((MOE_TM, tn), lambda j, b, be, nu: (blk(b, nu), j)),
    )
    return pl.pallas_call(
        _moe_up_kernel,
        grid_spec=grid_spec,
        out_shape=jax.ShapeDtypeStruct((n_slots, de), BF16),
        compiler_params=_cparams(("arbitrary", "arbitrary")),
        name="moe_up",
    )(block_e, n_used, xs, wg, wu, bg, bu)


def _moe_down_kernel(be_ref, nu_ref, act_ref, wd_ref, bd_ref, o_ref):
    del be_ref

    @pl.when(pl.program_id(1) < nu_ref[0])
    def _():
        y = jnp.dot(act_ref[...], wd_ref[0], preferred_element_type=F32) + bd_ref[0]
        o_ref[...] = _pack_bf16_pairs(y)


def _moe_down(block_e, n_used, act, wd, bd, tn):
    n_slots, de = act.shape
    d = wd.shape[2]
    nb = n_slots // MOE_TM

    def blk(b, nu):
        return jnp.minimum(b, nu[0] - 1)

    grid_spec = pltpu.PrefetchScalarGridSpec(
        num_scalar_prefetch=2,
        grid=(d // tn, nb),
        in_specs=[
            pl.BlockSpec((MOE_TM, de), lambda j, b, be, nu: (blk(b, nu), 0)),
            pl.BlockSpec((1, de, tn), lambda j, b, be, nu: (be[blk(b, nu)], 0, j)),
            pl.BlockSpec((1, 1, tn), lambda j, b, be, nu: (be[blk(b, nu)], 0, j)),
        ],
        out_specs=pl.BlockSpec((MOE_TM, tn // 2), lambda j, b, be, nu: (blk(b, nu), j)),
    )
    return pl.pallas_call(
        _moe_down_kernel,
        grid_spec=grid_spec,
        out_shape=jax.ShapeDtypeStruct((n_slots, d // 2), U32),
        compiler_params=_cparams(("arbitrary", "arbitrary")),
        name="moe_down",
    )(block_e, n_used, act, wd, bd)


def _combine_kernel(dest_ref, x_ref, w_ref, y_hbm, o_ref, ybuf, sem, *, tn):
    n = dest_ref.shape[0]

    def row_copy(i):
        t = lax.div(i, TOP_K)
        k = lax.rem(i, TOP_K)
        return pltpu.make_async_copy(y_hbm.at[pl.ds(dest_ref[i], 1)], ybuf.at[k, pl.ds(t, 1)], sem)

    def start(i, c):
        row_copy(i).start()
        return c

    def wait(i, c):
        row_copy(i).wait()
        return c

    lax.fori_loop(0, n, start, 0)
    lax.fori_loop(0, n, wait, 0)

    hw = tn // 2
    for j in range(x_ref.shape[1] // tn):
        lo = x_ref[:, j * tn:j * tn + hw]
        hi = x_ref[:, j * tn + hw:(j + 1) * tn]
        for k in range(TOP_K):
            yu = ybuf[k, :, j * hw:(j + 1) * hw]
            wk = w_ref[:, k:k + 1]
            lo = lo + wk * _unpack_lo(yu)
            hi = hi + wk * _unpack_hi(yu)
        o_ref[:, j * tn:j * tn + hw] = lo
        o_ref[:, j * tn + hw:(j + 1) * tn] = hi


def _combine(dest_flat, xmid, w_pad, y, tn):
    n, d = xmid.shape
    tb = _tile(n, 256, 8)
    return pl.pallas_call(
        functools.partial(_combine_kernel, tn=tn),
        grid=(n // tb,),
        in_specs=[
            pl.BlockSpec((tb * TOP_K,), lambda i: (i,), memory_space=pltpu.SMEM),
            pl.BlockSpec((tb, d), lambda i: (i, 0)),
            pl.BlockSpec((tb, LANES), lambda i: (i, 0)),
            pl.BlockSpec(memory_space=pl.ANY),
        ],
        out_specs=pl.BlockSpec((tb, d), lambda i: (i, 0)),
        out_shape=jax.ShapeDtypeStruct((n, d), F32),
        scratch_shapes=[pltpu.VMEM((TOP_K, tb, d // 2), U32), pltpu.SemaphoreType.DMA],
        compiler_params=_cparams(("arbitrary",)),
        name="moe_combine",
    )(dest_flat, xmid, w_pad, y)


def _rope_tables(s):
    t = jnp.arange(s, dtype=jnp.int32)
    pos = jnp.stack([t // GRID_W, t % GRID_W], axis=-1).astype(F32)
    axis_dims = HEAD_DIM // 2
    inv_freq = ROPE_THETA ** (-jnp.arange(0, axis_dims, 2, dtype=F32) / axis_dims)
    ang = pos[:, :, None] * inv_freq[None, None, :]
    cos, sin = jnp.cos(ang), jnp.sin(ang)
    cos_t = jnp.concatenate([cos[:, 0], cos[:, 0], cos[:, 1], cos[:, 1]], axis=-1)
    sin_t = jnp.concatenate([-sin[:, 0], sin[:, 0], -sin[:, 1], sin[:, 1]], axis=-1)
    return cos_t, sin_t


def _layer(x2, b, s, attn_norm_g, w_in, a_q_norm_g, a_k_norm_g, b_q_norm_g, b_k_norm_g, na_rel_bias,
           w_branch_a, w_branch_b, w_out, ffn_norm_g, w_router, b_router,
           w_gate, b_gate, w_up, b_up, w_down, b_down):
    n, d = x2.shape
    e, _, de = w_gate.shape
    rows = s // GRID_W
    assert s % (NA_CHUNK_ROWS * GRID_W) == 0 and rows >= 2 * NA_CHUNK_ROWS and rows >= NA_ROWS
    assert d % 512 == 0 and e <= LANES and e >= TOP_K

    src = {}
    o = 0
    for name, width in (("qa", A_Q_W), ("ka", A_KV_W), ("va", A_KV_W), ("qb", B_W), ("kb", B_W), ("vb", B_W),
                        ("ga", d), ("gb", d)):
        src[name] = (o, width)
        o += width
    off = {}
    o = 0
    pieces = []
    for name in ("ga", "gb", "qa", "qb", "kb", "vb", "ka", "va"):
        off[name] = o
        pieces.append(w_in[:, src[name][0]:src[name][0] + src[name][1]])
        o += src[name][1]
    w_perm = jnp.concatenate(pieces, axis=1).astype(BF16)
    proj = _inproj(x2, attn_norm_g.reshape(1, d), w_perm)

    cos_t, sin_t = _rope_tables(s)
    hd = lambda g: g.reshape(1, HEAD_DIM)
    qa, ka, vat, qb, kb = _prep(proj, cos_t, sin_t, hd(a_q_norm_g), hd(a_k_norm_g), hd(b_q_norm_g),
                                hd(b_k_norm_g), b, s, off)

    ya = _flash(qa.reshape(b, s, A_Q_W), ka.reshape(b, s, A_KV_W), vat)
    bias_tab = _na_bias_table(na_rel_bias, rows)
    yb = _na(qb.reshape(b, s, B_W), kb.reshape(b, s, B_W), proj.reshape(b, s, -1), off["vb"], bias_tab)

    wr = jnp.zeros((d, LANES), F32).at[:, :e].set(w_router)
    br = jnp.full((1, LANES), NEG, F32).at[0, :e].set(b_router)
    xmid, hu, logits = _merge(ya.reshape(n, A_Q_W), yb.reshape(n, B_W), proj, x2,
                              w_branch_a.astype(BF16), w_branch_b.astype(BF16), w_out.astype(BF16),
                              ffn_norm_g.reshape(1, d), wr, br, off)

    idx_p, w_p, rank_p, cnt_p = _router(logits)

    counts = cnt_p[0, :e]
    padded = (counts + MOE_TM - 1) // MOE_TM * MOE_TM
    pad_end = jnp.cumsum(padded)
    pad_start = pad_end - padded
    nk = n * TOP_K
    n_blocks = -(-nk // MOE_TM) + e
    n_slots = n_blocks * MOE_TM
    top_idx = idx_p[:, :TOP_K]
    dest_flat = (pad_start[top_idx] + rank_p[:, :TOP_K]).reshape(nk).astype(jnp.int32)
    block_start = jnp.arange(n_blocks, dtype=jnp.int32) * MOE_TM
    block_e = jnp.minimum(jnp.sum(block_start[:, None] >= pad_end[None, :], axis=1), e - 1).astype(jnp.int32)
    n_used = (pad_end[-1] // MOE_TM).astype(jnp.int32).reshape(1)

    xs = _dispatch(dest_flat, hu, n_slots)
    act = _moe_up(block_e, n_used, xs, w_gate.astype(BF16), w_up.astype(BF16),
                  b_gate.reshape(e, 1, de), b_up.reshape(e, 1, de))
    tn_d = _tile(d, 1024, 2 * LANES)
    y = _moe_down(block_e, n_used, act, w_down.astype(BF16), b_down.reshape(e, 1, d), tn_d)
    return _combine(dest_flat, xmid, w_p, y, tn_d)


def kernel(x, attn_norm_g, w_in, a_q_norm_g, a_k_norm_g, b_q_norm_g, b_k_norm_g, na_rel_bias, w_branch_a,
           w_branch_b, w_out, ffn_norm_g, w_router, b_router, w_gate, b_gate, w_up, b_up, w_down, b_down):
    b, s, d = x.shape
    x2 = x.reshape(b * s, d)
    for l in range(w_in.shape[0]):
        x2 = _layer(x2, b, s, attn_norm_g[l], w_in[l], a_q_norm_g[l], a_k_norm_g[l], b_q_norm_g[l],
                    b_k_norm_g[l], na_rel_bias[l], w_branch_a[l], w_branch_b[l], w_out[l], ffn_norm_g[l],
                    w_router[l], b_router[l], w_gate[l], b_gate[l], w_up[l], b_up[l], w_down[l], b_down[l])
    return x2.reshape(b, s, d)
```

```python
import functools
import math

import numpy as np
import jax
import jax.numpy as jnp
from jax import lax
from jax.experimental import pallas as pl
from jax.experimental.pallas import tpu as pltpu

GRID_W = 64
HEAD_DIM = 128
A_Q_HEADS = 8
A_KV_HEADS = 2
B_HEADS = 8
NA_ROWS = 8
NA_COLS = 16
ROPE_THETA = 10000.0
TOP_K = 4
SWIGLU_ALPHA = 1.702
SWIGLU_LIMIT = 7.0
EPS = 1e-6

A_Q_W = A_Q_HEADS * HEAD_DIM
A_KV_W = A_KV_HEADS * HEAD_DIM
B_W = B_HEADS * HEAD_DIM
GROUP = A_Q_HEADS // A_KV_HEADS

LANES = 128
SUBLANES = 8
NEG = -1e30
LOG2E = math.log2(math.e)
NA_CHUNK_ROWS = 8
NA_KEY_ROWS = 2 * NA_CHUNK_ROWS
MOE_TM = 512
DMA_UNROLL = 8
VMEM_LIMIT = 56 * 1024 * 1024
BOUND_MARGIN = 1.001
BOUND_LIMIT = 60.0

F32 = jnp.float32
BF16 = jnp.bfloat16
U32 = jnp.uint32
HI_MASK = np.uint32(0xFFFF0000)
NT_DIMS = (((1,), (1,)), ((), ()))


def _tile(n, pref, mult):
    if n <= pref:
        return n
    t = (pref // mult) * mult
    while t >= mult:
        if n % t == 0:
            return t
        t -= mult
    raise ValueError(f"no tile for {n} (pref {pref}, multiple of {mult})")


def _cparams(sem):
    return pltpu.CompilerParams(dimension_semantics=sem, vmem_limit_bytes=VMEM_LIMIT)


def _pack_rows(v, o_ref):
    m, d = v.shape
    half = d // 2
    r = half // LANES
    bits = lax.bitcast_convert_type(v.astype(BF16).astype(F32), U32)
    packed = (bits[:, :half] >> 16) | (bits[:, half:] & HI_MASK)
    for s in range(r):
        o_ref[pl.ds(s, m, stride=r), :] = packed[:, s * LANES:(s + 1) * LANES]


def _unpack_lo(u):
    return lax.bitcast_convert_type(u << 16, F32)


def _unpack_hi(u):
    return lax.bitcast_convert_type(u & HI_MASK, F32)


def _inproj_kernel(x_ref, g_ref, w_ref, o_ref, xn_ref):
    @pl.when(pl.program_id(1) == 0)
    def _():
        x = x_ref[...]
        ms = jnp.mean(x * x, axis=-1, keepdims=True)
        xn_ref[...] = (x * lax.rsqrt(ms + EPS) * g_ref[...]).astype(BF16)

    o_ref[...] = jnp.dot(xn_ref[...], w_ref[...], preferred_element_type=F32).astype(o_ref.dtype)


def _inproj(x2, g, w):
    n, d = x2.shape
    wcols = w.shape[1]
    tm = _tile(n, 1024, 8)
    tn = _tile(wcols, 512, LANES)
    return pl.pallas_call(
        _inproj_kernel,
        grid=(n // tm, wcols // tn),
        in_specs=[
            pl.BlockSpec((tm, d), lambda i, j: (i, 0)),
            pl.BlockSpec((1, d), lambda i, j: (0, 0)),
            pl.BlockSpec((d, tn), lambda i, j: (0, j)),
        ],
        out_specs=pl.BlockSpec((tm, tn), lambda i, j: (i, j)),
        out_shape=jax.ShapeDtypeStruct((n, wcols), BF16),
        scratch_shapes=[pltpu.VMEM((tm, d), BF16)],
        compiler_params=_cparams(("parallel", "arbitrary")),
        name="inproj",
    )(x2, g, w)


def _prep_kernel(qa_ref, ka_ref, va_ref, qb_ref, kb_ref, cos_ref, sin_ref,
                 gqa_ref, gka_ref, gqb_ref, gkb_ref,
                 qa_o, ka_o, vat_o, qb_o, kb_o, qmax_o, kmax_o):
    scale = HEAD_DIM ** -0.5
    scale2 = scale * LOG2E
    cos = cos_ref[...]
    sin = sin_ref[...]
    lane = lax.broadcasted_iota(jnp.int32, (1, HEAD_DIM), 1)
    first_half = (lane % (HEAD_DIM // 2)) < (HEAD_DIM // 4)

    def norm(xh, g):
        ms = jnp.mean(xh * xh, axis=-1, keepdims=True)
        return xh * lax.rsqrt(ms + EPS) * g

    def rope(xh):
        quarter = HEAD_DIM // 4
        partner = jnp.where(first_half, pltpu.roll(xh, HEAD_DIM - quarter, 1), pltpu.roll(xh, quarter, 1))
        return xh * cos + partner * sin

    def head(ref, h):
        return ref[:, h * HEAD_DIM:(h + 1) * HEAD_DIM].astype(F32)

    def sq_norm_max(y, running):
        yf = y.astype(F32)
        n2 = jnp.max(jnp.sum(yf * yf, axis=-1, keepdims=True), axis=0, keepdims=True)
        return jnp.maximum(running, n2)

    @pl.when(pl.program_id(0) == 0)
    def _():
        qmax_o[...] = jnp.zeros(qmax_o.shape, F32)
        kmax_o[...] = jnp.zeros(kmax_o.shape, F32)

    qmax = jnp.zeros((1, 1), F32)
    kmax = jnp.zeros((1, 1), F32)
    for h in range(A_Q_HEADS):
        y = (rope(norm(head(qa_ref, h), gqa_ref[...])) * scale2).astype(BF16)
        qa_o[:, h * HEAD_DIM:(h + 1) * HEAD_DIM] = y
        qmax = sq_norm_max(y, qmax)
    for h in range(A_KV_HEADS):
        y = rope(norm(head(ka_ref, h), gka_ref[...])).astype(BF16)
        ka_o[:, h * HEAD_DIM:(h + 1) * HEAD_DIM] = y
        kmax = sq_norm_max(y, kmax)
        vat_o[0, h] = head(va_ref, h).T.astype(BF16)
    qmax_o[...] = jnp.maximum(qmax_o[...], qmax)
    kmax_o[...] = jnp.maximum(kmax_o[...], kmax)
    for h in range(B_HEADS):
        qb_o[:, h * HEAD_DIM:(h + 1) * HEAD_DIM] = (norm(head(qb_ref, h), gqb_ref[...]) * scale).astype(BF16)
        kb_o[:, h * HEAD_DIM:(h + 1) * HEAD_DIM] = norm(head(kb_ref, h), gkb_ref[...]).astype(BF16)


def _prep(proj, cos_t, sin_t, gqa, gka, gqb, gkb, b, s, off):
    n = proj.shape[0]
    tm = _tile(s, 512, LANES)
    spb = s // tm

    def col(width, offset):
        assert offset % width == 0
        return pl.BlockSpec((tm, width), lambda i, o=offset // width: (i, o))

    gspec = pl.BlockSpec((1, HEAD_DIM), lambda i: (0, 0))
    tspec = pl.BlockSpec((tm, HEAD_DIM), lambda i: (i % spb, 0))
    stat = pl.BlockSpec((1, LANES), lambda i: (0, 0))
    return pl.pallas_call(
        _prep_kernel,
        grid=(n // tm,),
        in_specs=[col(A_Q_W, off["qa"]), col(A_KV_W, off["ka"]), col(A_KV_W, off["va"]),
                  col(B_W, off["qb"]), col(B_W, off["kb"]), tspec, tspec, gspec, gspec, gspec, gspec],
        out_specs=[
            pl.BlockSpec((tm, A_Q_W), lambda i: (i, 0)),
            pl.BlockSpec((tm, A_KV_W), lambda i: (i, 0)),
            pl.BlockSpec((1, A_KV_HEADS, HEAD_DIM, tm), lambda i: (i // spb, 0, 0, i % spb)),
            pl.BlockSpec((tm, B_W), lambda i: (i, 0)),
            pl.BlockSpec((tm, B_W), lambda i: (i, 0)),
            stat, stat,
        ],
        out_shape=[
            jax.ShapeDtypeStruct((n, A_Q_W), BF16),
            jax.ShapeDtypeStruct((n, A_KV_W), BF16),
            jax.ShapeDtypeStruct((b, A_KV_HEADS, HEAD_DIM, s), BF16),
            jax.ShapeDtypeStruct((n, B_W), BF16),
            jax.ShapeDtypeStruct((n, B_W), BF16),
            jax.ShapeDtypeStruct((1, LANES), F32),
            jax.ShapeDtypeStruct((1, LANES), F32),
        ],
        compiler_params=_cparams(("arbitrary",)),
        name="qk_prep",
    )(proj, proj, proj, proj, proj, cos_t, sin_t, gqa, gka, gqb, gkb)


def _flash_kernel(flag_ref, kmax_ref, q_ref, k_ref, vt_ref, o_ref, m_ref, l_ref, acc_ref):
    ki = pl.program_id(3)
    k = k_ref[0]
    vt = vt_ref[0, 0]

    def q_head(g):
        return q_ref[0, :, g * HEAD_DIM:(g + 1) * HEAD_DIM]

    @pl.when(ki == 0)
    def _():
        l_ref[...] = jnp.zeros(l_ref.shape, F32)
        acc_ref[...] = jnp.zeros(acc_ref.shape, F32)

    @pl.when(flag_ref[0] == 1)
    def _bounded():
        @pl.when(ki == 0)
        def _():
            ones = jnp.ones((SUBLANES, HEAD_DIM), F32)
            for g in range(GROUP):
                qf = q_head(g).astype(F32)
                n2 = lax.dot_general(ones, qf * qf, NT_DIMS, preferred_element_type=F32,
                                     precision=lax.Precision.HIGHEST)
                m_ref[g:g + 1, :] = jnp.sqrt(n2[0:1, :] * kmax_ref[0]) * BOUND_MARGIN

        for g in range(GROUP):
            st = lax.dot_general(k, q_head(g), NT_DIMS, preferred_element_type=F32)
            p = jnp.exp2(st - m_ref[g:g + 1, :])
            l_ref[g:g + 1, :] = l_ref[g:g + 1, :] + jnp.sum(p, axis=0, keepdims=True)
            acc_ref[g] = acc_ref[g] + jnp.dot(vt, p.astype(BF16), preferred_element_type=F32)

    @pl.when(flag_ref[0] == 0)
    def _online():
        @pl.when(ki == 0)
        def _():
            m_ref[...] = jnp.full(m_ref.shape, NEG, F32)

        for g in range(GROUP):
            st = lax.dot_general(k, q_head(g), NT_DIMS, preferred_element_type=F32)
            m_prev = m_ref[g:g + 1, :]
            m_new = jnp.maximum(m_prev, jnp.max(st, axis=0, keepdims=True))
            alpha = jnp.exp2(m_prev - m_new)
            p = jnp.exp2(st - m_new)
            l_ref[g:g + 1, :] = alpha * l_ref[g:g + 1, :] + jnp.sum(p, axis=0, keepdims=True)
            acc_ref[g] = acc_ref[g] * alpha + jnp.dot(vt, p.astype(BF16), preferred_element_type=F32)
            m_ref[g:g + 1, :] = m_new

    @pl.when(ki == pl.num_programs(3) - 1)
    def _():
        for g in range(GROUP):
            o = acc_ref[g] / l_ref[g:g + 1, :]
            o_ref[0, :, g * HEAD_DIM:(g + 1) * HEAD_DIM] = o.T.astype(o_ref.dtype)


def _flash(qa, ka, vat, flag, kmax):
    b, s, _ = qa.shape
    tq = _tile(s, 512, LANES)
    tk = _tile(s, 2048, LANES)
    gw = GROUP * HEAD_DIM
    grid_spec = pltpu.PrefetchScalarGridSpec(
        num_scalar_prefetch=2,
        grid=(b, A_KV_HEADS, s // tq, s // tk),
        in_specs=[
            pl.BlockSpec((1, tq, gw), lambda bi, h, qi, ki, f, km: (bi, qi, h)),
            pl.BlockSpec((1, tk, HEAD_DIM), lambda bi, h, qi, ki, f, km: (bi, ki, h)),
            pl.BlockSpec((1, 1, HEAD_DIM, tk), lambda bi, h, qi, ki, f, km: (bi, h, 0, ki)),
        ],
        out_specs=pl.BlockSpec((1, tq, gw), lambda bi, h, qi, ki, f, km: (bi, qi, h)),
        scratch_shapes=[pltpu.VMEM((SUBLANES, tq), F32), pltpu.VMEM((SUBLANES, tq), F32),
                        pltpu.VMEM((GROUP, HEAD_DIM, tq), F32)],
    )
    return pl.pallas_call(
        _flash_kernel,
        grid_spec=grid_spec,
        out_shape=jax.ShapeDtypeStruct((b, s, A_Q_W), BF16),
        compiler_params=_cparams(("parallel", "parallel", "parallel", "arbitrary")),
        name="gqa_flash",
    )(flag, kmax, qa, ka, vat)


def _na_kernel(q_ref, kp_ref, kc_ref, kn_ref, vp_ref, vc_ref, vn_ref, bias_ref, o_ref):
    half = (NA_CHUNK_ROWS // 2) * GRID_W
    q = q_ref[0]
    k = jnp.concatenate([kp_ref[0, half:, :], kc_ref[0], kn_ref[0, :half, :]], axis=0)
    v = jnp.concatenate([vp_ref[0, half:, :], vc_ref[0], vn_ref[0, :half, :]], axis=0)
    s = lax.dot_general(q, k, NT_DIMS, preferred_element_type=F32) + bias_ref[0, 0]
    m = jnp.max(s, axis=-1, keepdims=True)
    p = jnp.exp(s - m)
    l = jnp.sum(p, axis=-1, keepdims=True)
    o = jnp.dot(p.astype(BF16), v, preferred_element_type=F32) / l
    o_ref[0] = o.astype(o_ref.dtype)


def _na_bias_table(rel_bias, rows):
    n_heads = rel_bias.shape[0]
    n_chunks = rows // NA_CHUNK_ROWS
    cols = np.arange(GRID_W)
    c_start = np.clip(cols - NA_COLS // 2, 0, GRID_W - NA_COLS)
    valid_c = (cols[None, :] >= c_start[:, None]) & (cols[None, :] < c_start[:, None] + NA_COLS)
    col_off = np.clip(cols[None, :] - cols[:, None] + (NA_COLS - 1), 0, 2 * NA_COLS - 2)
    onehot = np.zeros((2 * NA_COLS - 1, GRID_W, GRID_W), np.float32)
    onehot[col_off, cols[:, None], cols[None, :]] = 1.0
    per_row_off = jnp.einsum("hdj,jqk->hdqk", rel_bias.astype(F32), onehot, precision=lax.Precision.HIGHEST)
    per_row_off = jnp.where(valid_c[None, None], per_row_off, NEG)
    pad = NA_CHUNK_ROWS // 2
    padded = jnp.pad(per_row_off, ((0, 0), (pad, pad), (0, 0), (0, 0)), constant_values=NEG)
    first = NA_ROWS - 1
    band = jnp.stack([padded[:, first - qr:first - qr + NA_KEY_ROWS] for qr in range(NA_CHUNK_ROWS)], axis=1)
    band = band.transpose(0, 1, 3, 2, 4).reshape(n_heads, NA_CHUNK_ROWS * GRID_W, NA_KEY_ROWS * GRID_W)

    qr_l = np.arange(NA_CHUNK_ROWS * GRID_W) // GRID_W
    kr_l = np.arange(NA_KEY_ROWS * GRID_W) // GRID_W
    tables = []
    for c in (0, min(1, n_chunks - 1), n_chunks - 1):
        qr = NA_CHUNK_ROWS * c + qr_l
        kr = NA_CHUNK_ROWS * c - NA_CHUNK_ROWS // 2 + kr_l
        r_start = np.clip(qr - NA_ROWS // 2, 0, rows - NA_ROWS)
        valid_r = (kr[None, :] >= r_start[:, None]) & (kr[None, :] < r_start[:, None] + NA_ROWS)
        tables.append(jnp.where(valid_r[None], band, NEG))
    return jnp.stack(tables, axis=0)


def _na(qb, kb, proj3, vb_off, bias_tab):
    b, s, _ = qb.shape
    tq = NA_CHUNK_ROWS * GRID_W
    n_chunks = s // tq
    vblk = vb_off // HEAD_DIM

    def variant(c):
        return jnp.where(c == 0, 0, jnp.where(c == n_chunks - 1, 2, 1))

    def kspec(shift, base):
        return pl.BlockSpec(
            (1, tq, HEAD_DIM),
            lambda bi, h, c: (bi, jnp.clip(c + shift, 0, n_chunks - 1), base + h))

    return pl.pallas_call(
        _na_kernel,
        grid=(b, B_HEADS, n_chunks),
        in_specs=[
            pl.BlockSpec((1, tq, HEAD_DIM), lambda bi, h, c: (bi, c, h)),
            kspec(-1, 0), kspec(0, 0), kspec(1, 0),
            kspec(-1, vblk), kspec(0, vblk), kspec(1, vblk),
            pl.BlockSpec((1, 1, tq, NA_KEY_ROWS * GRID_W), lambda bi, h, c: (variant(c), h, 0, 0)),
        ],
        out_specs=pl.BlockSpec((1, tq, HEAD_DIM), lambda bi, h, c: (bi, c, h)),
        out_shape=jax.ShapeDtypeStruct((b, s, B_W), BF16),
        compiler_params=_cparams(("parallel", "parallel", "arbitrary")),
        name="nbr_attn",
    )(qb, kb, kb, kb, proj3, proj3, proj3, bias_tab)


def _merge_kernel(ya_ref, yb_ref, ga_ref, gb_ref, x_ref, wpa_ref, wpb_ref, wout_ref, g2_ref, wr_ref, br_ref,
                  xmid_o, hu_o, logit_o):
    a = jnp.dot(ya_ref[...], wpa_ref[...], preferred_element_type=F32)
    bb = jnp.dot(yb_ref[...], wpb_ref[...], preferred_element_type=F32)
    m = jax.nn.sigmoid(ga_ref[...].astype(F32)) * a + jax.nn.sigmoid(gb_ref[...].astype(F32)) * bb
    xm = x_ref[...] + jnp.dot(m.astype(BF16), wout_ref[...], preferred_element_type=F32)
    xmid_o[...] = xm
    ms = jnp.mean(xm * xm, axis=-1, keepdims=True)
    h = xm * lax.rsqrt(ms + EPS) * g2_ref[...]
    logit_o[...] = jnp.dot(h, wr_ref[...], preferred_element_type=F32,
                           precision=lax.Precision.HIGHEST) + br_ref[...]
    _pack_rows(h, hu_o)


def _merge(ya, yb, proj, x2, wpa, wpb, wout, g2, wr, br, off):
    n, d = x2.shape
    r = d // (2 * LANES)
    tm = _tile(n, 256, 8)
    const = lambda i: (0, 0)
    wspec = lambda shape: pl.BlockSpec(shape, const, pipeline_mode=pl.Buffered(1))
    return pl.pallas_call(
        _merge_kernel,
        grid=(n // tm,),
        in_specs=[
            pl.BlockSpec((tm, A_Q_W), lambda i: (i, 0)),
            pl.BlockSpec((tm, B_W), lambda i: (i, 0)),
            pl.BlockSpec((tm, d), lambda i, o=off["ga"] // d: (i, o)),
            pl.BlockSpec((tm, d), lambda i, o=off["gb"] // d: (i, o)),
            pl.BlockSpec((tm, d), lambda i: (i, 0)),
            wspec((A_Q_W, d)), wspec((B_W, d)), wspec((d, d)),
            pl.BlockSpec((1, d), const),
            wspec((d, LANES)),
            pl.BlockSpec((1, LANES), const),
        ],
        out_specs=[
            pl.BlockSpec((tm, d), lambda i: (i, 0)),
            pl.BlockSpec((tm * r, LANES), lambda i: (i, 0)),
            pl.BlockSpec((tm, LANES), lambda i: (i, 0)),
        ],
        out_shape=[
            jax.ShapeDtypeStruct((n, d), F32),
            jax.ShapeDtypeStruct((n * r, LANES), U32),
            jax.ShapeDtypeStruct((n, LANES), F32),
        ],
        compiler_params=_cparams(("parallel",)),
        name="merge_outproj",
    )(ya, yb, proj, proj, x2, wpa, wpb, wout, g2, wr, br)


def _router_kernel(logit_ref, idx_o, w_o, rank_o, cnt_o, carry_ref):
    tm = logit_ref.shape[0]

    @pl.when(pl.program_id(0) == 0)
    def _():
        carry_ref[...] = jnp.zeros(carry_ref.shape, F32)

    l = logit_ref[...]
    lane = lax.broadcasted_iota(jnp.int32, (tm, LANES), 1)
    lane_f = lane.astype(F32)
    vals, idxs, hots = [], [], []
    for _ in range(TOP_K):
        mx = jnp.max(l, axis=-1, keepdims=True)
        idx = jnp.min(jnp.where(l == mx, lane_f, float(LANES)), axis=-1, keepdims=True)
        hot = lane_f == idx
        vals.append(mx)
        idxs.append(idx)
        hots.append(hot)
        l = jnp.where(hot, -3e38, l)

    exps = [jnp.exp(v - vals[0]) for v in vals]
    tot = exps[0]
    for e in exps[1:]:
        tot = tot + e

    sel = jnp.zeros((tm, LANES), F32)
    for hot in hots:
        sel = sel + hot.astype(F32)
    row = lax.broadcasted_iota(jnp.int32, (tm, tm), 0)
    colm = lax.broadcasted_iota(jnp.int32, (tm, tm), 1)
    lower = (colm < row).astype(BF16)
    before = jnp.dot(lower, sel.astype(BF16), preferred_element_type=F32) + carry_ref[...]

    idx_out = jnp.zeros((tm, LANES), F32)
    w_out = jnp.zeros((tm, LANES), F32)
    rank_out = jnp.zeros((tm, LANES), F32)
    for k in range(TOP_K):
        here = lane == k
        rank_k = jnp.sum(jnp.where(hots[k], before, 0.0), axis=-1, keepdims=True)
        idx_out = jnp.where(here, idxs[k], idx_out)
        w_out = jnp.where(here, exps[k] / tot, w_out)
        rank_out = jnp.where(here, rank_k, rank_out)
    idx_o[...] = idx_out.astype(jnp.int32)
    w_o[...] = w_out
    rank_o[...] = rank_out.astype(jnp.int32)
    carry_ref[...] = carry_ref[...] + jnp.sum(sel, axis=0, keepdims=True)
    cnt_o[...] = carry_ref[...].astype(jnp.int32)


def _router(logits):
    n = logits.shape[0]
    tm = _tile(n, 512, 8)
    blk = pl.BlockSpec((tm, LANES), lambda i: (i, 0))
    return pl.pallas_call(
        _router_kernel,
        grid=(n // tm,),
        in_specs=[blk],
        out_specs=[blk, blk, blk, pl.BlockSpec((1, LANES), lambda i: (0, 0))],
        out_shape=[
            jax.ShapeDtypeStruct((n, LANES), jnp.int32),
            jax.ShapeDtypeStruct((n, LANES), F32),
            jax.ShapeDtypeStruct((n, LANES), jnp.int32),
            jax.ShapeDtypeStruct((1, LANES), jnp.int32),
        ],
        scratch_shapes=[pltpu.VMEM((1, LANES), F32)],
        compiler_params=_cparams(("arbitrary",)),
        name="router_topk",
    )(logits)


def _dispatch_kernel(dest_ref, h_ref, xs_in, xs_out, sem, *, r):
    del xs_in
    n = dest_ref.shape[0]

    def tile_copy(i):
        src = pl.multiple_of(lax.div(i, TOP_K) * r, r)
        dst = pl.multiple_of(dest_ref[i] * r, r)
        return pltpu.make_async_copy(h_ref.at[pl.ds(src, r)], xs_out.at[pl.ds(dst, r)], sem)

    def start(i, c):
        tile_copy(i).start()
        return c

    def wait(i, c):
        tile_copy(i).wait()
        return c

    lax.fori_loop(0, n, start, 0, unroll=DMA_UNROLL)
    lax.fori_loop(0, n, wait, 0, unroll=DMA_UNROLL)


def _dispatch(dest_flat, hu, n_slots, r):
    n = hu.shape[0] // r
    tb = _tile(n, 256, 8)
    xs0 = jnp.zeros((n_slots * r, LANES), U32)
    return pl.pallas_call(
        functools.partial(_dispatch_kernel, r=r),
        grid=(n // tb,),
        in_specs=[
            pl.BlockSpec((tb * TOP_K,), lambda i: (i,), memory_space=pltpu.SMEM),
            pl.BlockSpec((tb * r, LANES), lambda i: (i, 0)),
            pl.BlockSpec(memory_space=pl.ANY),
        ],
        out_specs=pl.BlockSpec(memory_space=pl.ANY),
        out_shape=jax.ShapeDtypeStruct((n_slots * r, LANES), U32),
        scratch_shapes=[pltpu.SemaphoreType.DMA],
        input_output_aliases={2: 0},
        compiler_params=_cparams(("arbitrary",)),
        name="moe_dispatch",
    )(dest_flat, hu, xs0)


def _used_block(b, nu):
    return jnp.minimum(b, nu[0] - 1)


def _moe_up_kernel(be_ref, nu_ref, xs_ref, wg_ref, wu_ref, bg_ref, bu_ref, o_ref, *, r):
    del be_ref

    @pl.when(pl.program_id(1) < nu_ref[0])
    def _():
        half = r * LANES
        words = [xs_ref[pl.ds(s, MOE_TM, stride=r), :] for s in range(r)]
        lo = jnp.concatenate([_unpack_lo(u).astype(BF16) for u in words], axis=1)
        hi = jnp.concatenate([_unpack_hi(u).astype(BF16) for u in words], axis=1)

        def proj(w_ref, b_ref):
            return (jnp.dot(lo, w_ref[0, :half, :], preferred_element_type=F32)
                    + jnp.dot(hi, w_ref[0, half:, :], preferred_element_type=F32) + b_ref[0])

        gate = jnp.minimum(proj(wg_ref, bg_ref), SWIGLU_LIMIT)
        up = jnp.clip(proj(wu_ref, bu_ref), -SWIGLU_LIMIT, SWIGLU_LIMIT)
        o_ref[...] = (gate * jax.nn.sigmoid(SWIGLU_ALPHA * gate) * (up + 1.0)).astype(o_ref.dtype)

    @pl.when(pl.program_id(1) >= nu_ref[0])
    def _():
        o_ref[...] = jnp.zeros(o_ref.shape, o_ref.dtype)


def _moe_up(block_e, n_used, xs, wg, wu, bg, bu, r):
    n_slots = xs.shape[0] // r
    e, d, de = wg.shape
    tn = _tile(de, 1024, LANES)
    nb = n_slots // MOE_TM
    grid_spec = pltpu.PrefetchScalarGridSpec(
        num_scalar_prefetch=2,
        grid=(de // tn, nb),
        in_specs=[
            pl.BlockSpec((MOE_TM * r, LANES), lambda j, b, be, nu: (_used_block(b, nu), 0)),
            pl.BlockSpec((1, d, tn), lambda j, b, be, nu: (be[_used_block(b, nu)], 0, j)),
            pl.BlockSpec((1, d, tn), lambda j, b, be, nu: (be[_used_block(b, nu)], 0, j)),
            pl.BlockSpec((1, 1, tn), lambda j, b, be, nu: (be[_used_block(b, nu)], 0, j)),
            pl.BlockSpec((1, 1, tn), lambda j, b, be, nu: (be[_used_block(b, nu)], 0, j)),
        ],
        out_specs=pl.BlockSpec((MOE_TM, tn), lambda j, b, be, nu: (b, j)),
    )
    return pl.pallas_call(
        functools.partial(_moe_up_kernel, r=r),
        grid_spec=grid_spec,
        out_shape=jax.ShapeDtypeStruct((n_slots, de), BF16),
        compiler_params=_cparams(("arbitrary", "arbitrary")),
        name="moe_up",
    )(block_e, n_used, xs, wg, wu, bg, bu)


def _moe_down_kernel(be_ref, nu_ref, act_ref, wd_ref, bd_ref, o_ref):
    del be_ref

    @pl.when(pl.program_id(0) < nu_ref[0])
    def _():
        y = jnp.dot(act_ref[...], wd_ref[0], preferred_element_type=F32) + bd_ref[0]
        _pack_rows(y, o_ref)

    @pl.when(pl.program_id(0) >= nu_ref[0])
    def _():
        o_ref[...] = jnp.zeros(o_ref.shape, o_ref.dtype)


def _moe_down(block_e, n_used, act, wd, bd):
    n_slots, de = act.shape
    d = wd.shape[2]
    r = d // (2 * LANES)
    nb = n_slots // MOE_TM
    grid_spec = pltpu.PrefetchScalarGridSpec(
        num_scalar_prefetch=2,
        grid=(nb,),
        in_specs=[
            pl.BlockSpec((MOE_TM, de), lambda b, be, nu: (_used_block(b, nu), 0)),
            pl.BlockSpec((1, de, d), lambda b, be, nu: (be[_used_block(b, nu)], 0, 0)),
            pl.BlockSpec((1, 1, d), lambda b, be, nu: (be[_used_block(b, nu)], 0, 0)),
        ],
        out_specs=pl.BlockSpec((MOE_TM * r, LANES), lambda b, be, nu: (b, 0)),
    )
    return pl.pallas_call(
        _moe_down_kernel,
        grid_spec=grid_spec,
        out_shape=jax.ShapeDtypeStruct((n_slots * r, LANES), U32),
        compiler_params=_cparams(("arbitrary",)),
        name="moe_down",
    )(block_e, n_used, act, wd, bd)


def _combine_kernel(dest_ref, x_ref, w_ref, y_hbm, o_ref, ybuf, sem, *, r):
    n = dest_ref.shape[0]
    tb, d = x_ref.shape
    half = d // 2

    def tile_copy(i):
        src = pl.multiple_of(dest_ref[i] * r, r)
        dst = pl.multiple_of(lax.div(i, TOP_K) * r, r)
        return pltpu.make_async_copy(y_hbm.at[pl.ds(src, r)], ybuf.at[lax.rem(i, TOP_K), pl.ds(dst, r)], sem)

    def start(i, c):
        tile_copy(i).start()
        return c

    def wait(i, c):
        tile_copy(i).wait()
        return c

    lax.fori_loop(0, n, start, 0, unroll=DMA_UNROLL)
    lax.fori_loop(0, n, wait, 0, unroll=DMA_UNROLL)

    weights = [w_ref[:, k:k + 1] for k in range(TOP_K)]
    for s in range(r):
        lo = x_ref[:, s * LANES:(s + 1) * LANES]
        hi = x_ref[:, half + s * LANES:half + (s + 1) * LANES]
        for k in range(TOP_K):
            yu = ybuf[k, pl.ds(s, tb, stride=r), :]
            lo = lo + weights[k] * _unpack_lo(yu)
            hi = hi + weights[k] * _unpack_hi(yu)
        o_ref[:, s * LANES:(s + 1) * LANES] = lo
        o_ref[:, half + s * LANES:half + (s + 1) * LANES] = hi


def _combine(dest_flat, xmid, w_pad, y, r):
    n, d = xmid.shape
    tb = _tile(n, 256, 8)
    return pl.pallas_call(
        functools.partial(_combine_kernel, r=r),
        grid=(n // tb,),
        in_specs=[
            pl.BlockSpec((tb * TOP_K,), lambda i: (i,), memory_space=pltpu.SMEM),
            pl.BlockSpec((tb, d), lambda i: (i, 0)),
            pl.BlockSpec((tb, LANES), lambda i: (i, 0)),
            pl.BlockSpec(memory_space=pl.ANY),
        ],
        out_specs=pl.BlockSpec((tb, d), lambda i: (i, 0)),
        out_shape=jax.ShapeDtypeStruct((n, d), F32),
        scratch_shapes=[pltpu.VMEM((TOP_K, tb * r, LANES), U32), pltpu.SemaphoreType.DMA],
        compiler_params=_cparams(("arbitrary",)),
        name="moe_combine",
    )(dest_flat, xmid, w_pad, y)


def _rope_tables(s):
    t = jnp.arange(s, dtype=jnp.int32)
    pos = jnp.stack([t // GRID_W, t % GRID_W], axis=-1).astype(F32)
    axis_dims = HEAD_DIM // 2
    inv_freq = ROPE_THETA ** (-jnp.arange(0, axis_dims, 2, dtype=F32) / axis_dims)
    ang = pos[:, :, None] * inv_freq[None, None, :]
    cos, sin = jnp.cos(ang), jnp.sin(ang)
    cos_t = jnp.concatenate([cos[:, 0], cos[:, 0], cos[:, 1], cos[:, 1]], axis=-1)
    sin_t = jnp.concatenate([-sin[:, 0], sin[:, 0], -sin[:, 1], sin[:, 1]], axis=-1)
    return cos_t, sin_t


def _layer(x2, b, s, attn_norm_g, w_in, a_q_norm_g, a_k_norm_g, b_q_norm_g, b_k_norm_g, na_rel_bias,
           w_branch_a, w_branch_b, w_out, ffn_norm_g, w_router, b_router,
           w_gate, b_gate, w_up, b_up, w_down, b_down):
    n, d = x2.shape
    e, _, de = w_gate.shape
    rows = s // GRID_W
    r = d // (2 * LANES)
    assert s % (NA_CHUNK_ROWS * GRID_W) == 0 and rows >= 2 * NA_CHUNK_ROWS and rows >= NA_ROWS
    assert d % (2 * LANES * SUBLANES) == 0 and e <= LANES and e >= TOP_K

    src = {}
    o = 0
    for name, width in (("qa", A_Q_W), ("ka", A_KV_W), ("va", A_KV_W), ("qb", B_W), ("kb", B_W), ("vb", B_W),
                        ("ga", d), ("gb", d)):
        src[name] = (o, width)
        o += width
    off = {}
    o = 0
    pieces = []
    for name in ("ga", "gb", "qa", "qb", "kb", "vb", "ka", "va"):
        off[name] = o
        pieces.append(w_in[:, src[name][0]:src[name][0] + src[name][1]])
        o += src[name][1]
    w_perm = jnp.concatenate(pieces, axis=1).astype(BF16)
    proj = _inproj(x2, attn_norm_g.reshape(1, d), w_perm)

    cos_t, sin_t = _rope_tables(s)
    hd = lambda g: g.reshape(1, HEAD_DIM)
    qa, ka, vat, qb, kb, qmax, kmax = _prep(proj, cos_t, sin_t, hd(a_q_norm_g), hd(a_k_norm_g),
                                            hd(b_q_norm_g), hd(b_k_norm_g), b, s, off)

    bound_max = jnp.sqrt(qmax[0, 0] * kmax[0, 0]) * BOUND_MARGIN
    flag = (bound_max <= BOUND_LIMIT).astype(jnp.int32).reshape(1)
    ya = _flash(qa.reshape(b, s, A_Q_W), ka.reshape(b, s, A_KV_W), vat, flag, kmax[0, :1])
    bias_tab = _na_bias_table(na_rel_bias, rows)
    yb = _na(qb.reshape(b, s, B_W), kb.reshape(b, s, B_W), proj.reshape(b, s, -1), off["vb"], bias_tab)

    wr = jnp.zeros((d, LANES), F32).at[:, :e].set(w_router)
    br = jnp.full((1, LANES), NEG, F32).at[0, :e].set(b_router)
    xmid, hu, logits = _merge(ya.reshape(n, A_Q_W), yb.reshape(n, B_W), proj, x2,
                              w_branch_a.astype(BF16), w_branch_b.astype(BF16), w_out.astype(BF16),
                              ffn_norm_g.reshape(1, d), wr, br, off)

    idx_p, w_p, rank_p, cnt_p = _router(logits)

    counts = cnt_p[0, :e]
    padded = (counts + MOE_TM - 1) // MOE_TM * MOE_TM
    pad_end = jnp.cumsum(padded)
    pad_start = pad_end - padded
    nk = n * TOP_K
    n_blocks = -(-nk // MOE_TM) + e
    n_slots = n_blocks * MOE_TM
    top_idx = idx_p[:, :TOP_K]
    dest_flat = (pad_start[top_idx] + rank_p[:, :TOP_K]).reshape(nk).astype(jnp.int32)
    block_start = jnp.arange(n_blocks, dtype=jnp.int32) * MOE_TM
    block_e = jnp.minimum(jnp.sum(block_start[:, None] >= pad_end[None, :], axis=1), e - 1).astype(jnp.int32)
    n_used = (pad_end[-1] // MOE_TM).astype(jnp.int32).reshape(1)

    xs = _dispatch(dest_flat, hu, n_slots, r)
    act = _moe_up(block_e, n_used, xs, w_gate.astype(BF16), w_up.astype(BF16),
                  b_gate.reshape(e, 1, de), b_up.reshape(e, 1, de), r)
    y = _moe_down(block_e, n_used, act, w_down.astype(BF16), b_down.reshape(e, 1, d))
    return _combine(dest_flat, xmid, w_p, y, r)


def kernel(x, attn_norm_g, w_in, a_q_norm_g, a_k_norm_g, b_q_norm_g, b_k_norm_g, na_rel_bias, w_branch_a,
           w_branch_b, w_out, ffn_norm_g, w_router, b_router, w_gate, b_gate, w_up, b_up, w_down, b_down):
    b, s, d = x.shape
    x2 = x.reshape(b * s, d)
    for l in range(w_in.shape[0]):
        x2 = _layer(x2, b, s, attn_norm_g[l], w_in[l], a_q_norm_g[l], a_k_norm_g[l], b_q_norm_g[l],
                    b_k_norm_g[l], na_rel_bias[l], w_branch_a[l], w_branch_b[l], w_out[l], ffn_norm_g[l],
                    w_router[l], b_router[l], w_gate[l], b_gate[l], w_up[l], b_up[l], w_down[l], b_down[l])
    return x2.reshape(b, s, d)
```

```python
import functools
import math

import numpy as np
import jax
import jax.numpy as jnp
from jax import lax
from jax.experimental import pallas as pl
from jax.experimental.pallas import tpu as pltpu

GRID_W = 64
HEAD_DIM = 128
A_Q_HEADS = 8
A_KV_HEADS = 2
B_HEADS = 8
NA_ROWS = 8
NA_COLS = 16
ROPE_THETA = 10000.0
TOP_K = 4
SWIGLU_ALPHA = 1.702
SWIGLU_LIMIT = 7.0
EPS = 1e-6

A_Q_W = A_Q_HEADS * HEAD_DIM
A_KV_W = A_KV_HEADS * HEAD_DIM
B_W = B_HEADS * HEAD_DIM
GROUP = A_Q_HEADS // A_KV_HEADS

LANES = 128
SUBLANES = 8
NEG = -1e30
LOG2E = math.log2(math.e)
NA_CHUNK_ROWS = 8
NA_KEY_ROWS = 2 * NA_CHUNK_ROWS
MOE_TM = 512
DMA_UNROLL = 8
VMEM_LIMIT = 56 * 1024 * 1024
BOUND_MARGIN = 1.001
BOUND_LIMIT = 60.0

F32 = jnp.float32
BF16 = jnp.bfloat16
U32 = jnp.uint32
HI_MASK = np.uint32(0xFFFF0000)
NT_DIMS = (((1,), (1,)), ((), ()))


def _tile(n, pref, mult):
    if n <= pref:
        return n
    t = (pref // mult) * mult
    while t >= mult:
        if n % t == 0:
            return t
        t -= mult
    raise ValueError(f"no tile for {n} (pref {pref}, multiple of {mult})")


def _cparams(sem):
    return pltpu.CompilerParams(dimension_semantics=sem, vmem_limit_bytes=VMEM_LIMIT)


def _pack_rows(v, o_ref):
    m, d = v.shape
    half = d // 2
    r = half // LANES
    bits = lax.bitcast_convert_type(v.astype(BF16).astype(F32), U32)
    packed = (bits[:, :half] >> 16) | (bits[:, half:] & HI_MASK)
    for s in range(r):
        o_ref[pl.ds(s, m, stride=r), :] = packed[:, s * LANES:(s + 1) * LANES]


def _unpack_lo(u):
    return lax.bitcast_convert_type(u << 16, F32)


def _unpack_hi(u):
    return lax.bitcast_convert_type(u & HI_MASK, F32)


def _inproj_kernel(x_ref, g_ref, w_ref, o_ref, xn_ref):
    @pl.when(pl.program_id(1) == 0)
    def _():
        x = x_ref[...]
        ms = jnp.mean(x * x, axis=-1, keepdims=True)
        xn_ref[...] = (x * lax.rsqrt(ms + EPS) * g_ref[...]).astype(BF16)

    o_ref[...] = jnp.dot(xn_ref[...], w_ref[...], preferred_element_type=F32).astype(o_ref.dtype)


def _inproj(x2, g, w):
    n, d = x2.shape
    wcols = w.shape[1]
    tm = _tile(n, 1024, 8)
    tn = _tile(wcols, 512, LANES)
    return pl.pallas_call(
        _inproj_kernel,
        grid=(n // tm, wcols // tn),
        in_specs=[
            pl.BlockSpec((tm, d), lambda i, j: (i, 0)),
            pl.BlockSpec((1, d), lambda i, j: (0, 0)),
            pl.BlockSpec((d, tn), lambda i, j: (0, j)),
        ],
        out_specs=pl.BlockSpec((tm, tn), lambda i, j: (i, j)),
        out_shape=jax.ShapeDtypeStruct((n, wcols), BF16),
        scratch_shapes=[pltpu.VMEM((tm, d), BF16)],
        compiler_params=_cparams(("parallel", "arbitrary")),
        name="inproj",
    )(x2, g, w)


def _prep_kernel(qa_ref, ka_ref, va_ref, qb_ref, kb_ref, cos_ref, sin_ref,
                 gqa_ref, gka_ref, gqb_ref, gkb_ref,
                 qa_o, ka_o, vat_o, qb_o, kb_o, qmax_o, kmax_o):
    scale = HEAD_DIM ** -0.5
    scale2 = scale * LOG2E
    cos = cos_ref[...]
    sin = sin_ref[...]
    lane = lax.broadcasted_iota(jnp.int32, (1, HEAD_DIM), 1)
    first_half = (lane % (HEAD_DIM // 2)) < (HEAD_DIM // 4)

    def norm(xh, g):
        ms = jnp.mean(xh * xh, axis=-1, keepdims=True)
        return xh * lax.rsqrt(ms + EPS) * g

    def rope(xh):
        quarter = HEAD_DIM // 4
        partner = jnp.where(first_half, pltpu.roll(xh, HEAD_DIM - quarter, 1), pltpu.roll(xh, quarter, 1))
        return xh * cos + partner * sin

    def head(ref, h):
        return ref[:, h * HEAD_DIM:(h + 1) * HEAD_DIM].astype(F32)

    def sq_norm_max(y, running):
        yf = y.astype(F32)
        n2 = jnp.max(jnp.sum(yf * yf, axis=-1, keepdims=True), axis=0, keepdims=True)
        return jnp.maximum(running, n2)

    @pl.when(pl.program_id(0) == 0)
    def _():
        qmax_o[...] = jnp.zeros(qmax_o.shape, F32)
        kmax_o[...] = jnp.zeros(kmax_o.shape, F32)

    qmax = jnp.zeros((1, 1), F32)
    kmax = jnp.zeros((1, 1), F32)
    for h in range(A_Q_HEADS):
        y = (rope(norm(head(qa_ref, h), gqa_ref[...])) * scale2).astype(BF16)
        qa_o[:, h * HEAD_DIM:(h + 1) * HEAD_DIM] = y
        qmax = sq_norm_max(y, qmax)
    for h in range(A_KV_HEADS):
        y = rope(norm(head(ka_ref, h), gka_ref[...])).astype(BF16)
        ka_o[:, h * HEAD_DIM:(h + 1) * HEAD_DIM] = y
        kmax = sq_norm_max(y, kmax)
        vat_o[0, h] = head(va_ref, h).T.astype(BF16)
    qmax_o[...] = jnp.maximum(qmax_o[...], qmax)
    kmax_o[...] = jnp.maximum(kmax_o[...], kmax)
    for h in range(B_HEADS):
        qb_o[:, h * HEAD_DIM:(h + 1) * HEAD_DIM] = (norm(head(qb_ref, h), gqb_ref[...]) * scale).astype(BF16)
        kb_o[:, h * HEAD_DIM:(h + 1) * HEAD_DIM] = norm(head(kb_ref, h), gkb_ref[...]).astype(BF16)


def _prep(proj, cos_t, sin_t, gqa, gka, gqb, gkb, b, s, off):
    n = proj.shape[0]
    tm = _tile(s, 512, LANES)
    spb = s // tm

    def col(width, offset):
        assert offset % width == 0
        return pl.BlockSpec((tm, width), lambda i, o=offset // width: (i, o))

    gspec = pl.BlockSpec((1, HEAD_DIM), lambda i: (0, 0))
    tspec = pl.BlockSpec((tm, HEAD_DIM), lambda i: (i % spb, 0))
    stat = pl.BlockSpec((1, LANES), lambda i: (0, 0))
    return pl.pallas_call(
        _prep_kernel,
        grid=(n // tm,),
        in_specs=[col(A_Q_W, off["qa"]), col(A_KV_W, off["ka"]), col(A_KV_W, off["va"]),
                  col(B_W, off["qb"]), col(B_W, off["kb"]), tspec, tspec, gspec, gspec, gspec, gspec],
        out_specs=[
            pl.BlockSpec((tm, A_Q_W), lambda i: (i, 0)),
            pl.BlockSpec((tm, A_KV_W), lambda i: (i, 0)),
            pl.BlockSpec((1, A_KV_HEADS, HEAD_DIM, tm), lambda i: (i // spb, 0, 0, i % spb)),
            pl.BlockSpec((tm, B_W), lambda i: (i, 0)),
            pl.BlockSpec((tm, B_W), lambda i: (i, 0)),
            stat, stat,
        ],
        out_shape=[
            jax.ShapeDtypeStruct((n, A_Q_W), BF16),
            jax.ShapeDtypeStruct((n, A_KV_W), BF16),
            jax.ShapeDtypeStruct((b, A_KV_HEADS, HEAD_DIM, s), BF16),
            jax.ShapeDtypeStruct((n, B_W), BF16),
            jax.ShapeDtypeStruct((n, B_W), BF16),
            jax.ShapeDtypeStruct((1, LANES), F32),
            jax.ShapeDtypeStruct((1, LANES), F32),
        ],
        compiler_params=_cparams(("arbitrary",)),
        name="qk_prep",
    )(proj, proj, proj, proj, proj, cos_t, sin_t, gqa, gka, gqb, gkb)


def _flash_kernel(flag_ref, kmax_ref, q_ref, k_ref, vt_ref, o_ref, m_ref, l_ref, acc_ref):
    ki = pl.program_id(3)
    k = k_ref[0]
    vt = vt_ref[0, 0]

    def q_head(g):
        return q_ref[0, :, g * HEAD_DIM:(g + 1) * HEAD_DIM]

    @pl.when(ki == 0)
    def _():
        l_ref[...] = jnp.zeros(l_ref.shape, F32)
        acc_ref[...] = jnp.zeros(acc_ref.shape, F32)

    @pl.when(flag_ref[0] == 1)
    def _bounded():
        @pl.when(ki == 0)
        def _():
            ones = jnp.ones((SUBLANES, HEAD_DIM), F32)
            for g in range(GROUP):
                qf = q_head(g).astype(F32)
                n2 = lax.dot_general(ones, qf * qf, NT_DIMS, preferred_element_type=F32,
                                     precision=lax.Precision.HIGHEST)
                m_ref[g:g + 1, :] = jnp.sqrt(n2[0:1, :] * kmax_ref[0]) * BOUND_MARGIN

        for g in range(GROUP):
            st = lax.dot_general(k, q_head(g), NT_DIMS, preferred_element_type=F32)
            p = jnp.exp2(st - m_ref[g:g + 1, :])
            l_ref[g:g + 1, :] = l_ref[g:g + 1, :] + jnp.sum(p, axis=0, keepdims=True)
            acc_ref[g] = acc_ref[g] + jnp.dot(vt, p.astype(BF16), preferred_element_type=F32)

    @pl.when(flag_ref[0] == 0)
    def _online():
        @pl.when(ki == 0)
        def _():
            m_ref[...] = jnp.full(m_ref.shape, NEG, F32)

        for g in range(GROUP):
            st = lax.dot_general(k, q_head(g), NT_DIMS, preferred_element_type=F32)
            m_prev = m_ref[g:g + 1, :]
            m_new = jnp.maximum(m_prev, jnp.max(st, axis=0, keepdims=True))
            alpha = jnp.exp2(m_prev - m_new)
            p = jnp.exp2(st - m_new)
            l_ref[g:g + 1, :] = alpha * l_ref[g:g + 1, :] + jnp.sum(p, axis=0, keepdims=True)
            acc_ref[g] = acc_ref[g] * alpha + jnp.dot(vt, p.astype(BF16), preferred_element_type=F32)
            m_ref[g:g + 1, :] = m_new

    @pl.when(ki == pl.num_programs(3) - 1)
    def _():
        for g in range(GROUP):
            o = acc_ref[g] / l_ref[g:g + 1, :]
            o_ref[0, :, g * HEAD_DIM:(g + 1) * HEAD_DIM] = o.T.astype(o_ref.dtype)


def _flash(qa, ka, vat, flag, kmax):
    b, s, _ = qa.shape
    tq = _tile(s, 512, LANES)
    tk = _tile(s, 2048, LANES)
    gw = GROUP * HEAD_DIM
    grid_spec = pltpu.PrefetchScalarGridSpec(
        num_scalar_prefetch=2,
        grid=(b, A_KV_HEADS, s // tq, s // tk),
        in_specs=[
            pl.BlockSpec((1, tq, gw), lambda bi, h, qi, ki, f, km: (bi, qi, h)),
            pl.BlockSpec((1, tk, HEAD_DIM), lambda bi, h, qi, ki, f, km: (bi, ki, h)),
            pl.BlockSpec((1, 1, HEAD_DIM, tk), lambda bi, h, qi, ki, f, km: (bi, h, 0, ki)),
        ],
        out_specs=pl.BlockSpec((1, tq, gw), lambda bi, h, qi, ki, f, km: (bi, qi, h)),
        scratch_shapes=[pltpu.VMEM((SUBLANES, tq), F32), pltpu.VMEM((SUBLANES, tq), F32),
                        pltpu.VMEM((GROUP, HEAD_DIM, tq), F32)],
    )
    return pl.pallas_call(
        _flash_kernel,
        grid_spec=grid_spec,
        out_shape=jax.ShapeDtypeStruct((b, s, A_Q_W), BF16),
        compiler_params=_cparams(("parallel", "parallel", "parallel", "arbitrary")),
        name="gqa_flash",
    )(flag, kmax, qa, ka, vat)


def _na_kernel(q_ref, kp_ref, kc_ref, kn_ref, vp_ref, vc_ref, vn_ref, bias_ref, o_ref):
    half = (NA_CHUNK_ROWS // 2) * GRID_W
    q = q_ref[0]
    k = jnp.concatenate([kp_ref[0, half:, :], kc_ref[0], kn_ref[0, :half, :]], axis=0)
    v = jnp.concatenate([vp_ref[0, half:, :], vc_ref[0], vn_ref[0, :half, :]], axis=0)
    s = lax.dot_general(q, k, NT_DIMS, preferred_element_type=F32) + bias_ref[0, 0]
    m = jnp.max(s, axis=-1, keepdims=True)
    p = jnp.exp(s - m)
    l = jnp.sum(p, axis=-1, keepdims=True)
    o = jnp.dot(p.astype(BF16), v, preferred_element_type=F32) / l
    o_ref[0] = o.astype(o_ref.dtype)


def _na_bias_table(rel_bias, rows):
    n_heads = rel_bias.shape[0]
    n_chunks = rows // NA_CHUNK_ROWS
    cols = np.arange(GRID_W)
    c_start = np.clip(cols - NA_COLS // 2, 0, GRID_W - NA_COLS)
    valid_c = (cols[None, :] >= c_start[:, None]) & (cols[None, :] < c_start[:, None] + NA_COLS)
    col_off = np.clip(cols[None, :] - cols[:, None] + (NA_COLS - 1), 0, 2 * NA_COLS - 2)
    onehot = np.zeros((2 * NA_COLS - 1, GRID_W, GRID_W), np.float32)
    onehot[col_off, cols[:, None], cols[None, :]] = 1.0
    per_row_off = jnp.einsum("hdj,jqk->hdqk", rel_bias.astype(F32), onehot, precision=lax.Precision.HIGHEST)
    per_row_off = jnp.where(valid_c[None, None], per_row_off, NEG)
    pad = NA_CHUNK_ROWS // 2
    padded = jnp.pad(per_row_off, ((0, 0), (pad, pad), (0, 0), (0, 0)), constant_values=NEG)
    first = NA_ROWS - 1
    band = jnp.stack([padded[:, first - qr:first - qr + NA_KEY_ROWS] for qr in range(NA_CHUNK_ROWS)], axis=1)
    band = band.transpose(0, 1, 3, 2, 4).reshape(n_heads, NA_CHUNK_ROWS * GRID_W, NA_KEY_ROWS * GRID_W)

    qr_l = np.arange(NA_CHUNK_ROWS * GRID_W) // GRID_W
    kr_l = np.arange(NA_KEY_ROWS * GRID_W) // GRID_W
    tables = []
    for c in (0, min(1, n_chunks - 1), n_chunks - 1):
        qr = NA_CHUNK_ROWS * c + qr_l
        kr = NA_CHUNK_ROWS * c - NA_CHUNK_ROWS // 2 + kr_l
        r_start = np.clip(qr - NA_ROWS // 2, 0, rows - NA_ROWS)
        valid_r = (kr[None, :] >= r_start[:, None]) & (kr[None, :] < r_start[:, None] + NA_ROWS)
        tables.append(jnp.where(valid_r[None], band, NEG))
    return jnp.stack(tables, axis=0)


def _na(qb, kb, proj3, vb_off, bias_tab):
    b, s, _ = qb.shape
    tq = NA_CHUNK_ROWS * GRID_W
    n_chunks = s // tq
    vblk = vb_off // HEAD_DIM

    def variant(c):
        return jnp.where(c == 0, 0, jnp.where(c == n_chunks - 1, 2, 1))

    def kspec(shift, base):
        return pl.BlockSpec(
            (1, tq, HEAD_DIM),
            lambda bi, h, c: (bi, jnp.clip(c + shift, 0, n_chunks - 1), base + h))

    return pl.pallas_call(
        _na_kernel,
        grid=(b, B_HEADS, n_chunks),
        in_specs=[
            pl.BlockSpec((1, tq, HEAD_DIM), lambda bi, h, c: (bi, c, h)),
            kspec(-1, 0), kspec(0, 0), kspec(1, 0),
            kspec(-1, vblk), kspec(0, vblk), kspec(1, vblk),
            pl.BlockSpec((1, 1, tq, NA_KEY_ROWS * GRID_W), lambda bi, h, c: (variant(c), h, 0, 0)),
        ],
        out_specs=pl.BlockSpec((1, tq, HEAD_DIM), lambda bi, h, c: (bi, c, h)),
        out_shape=jax.ShapeDtypeStruct((b, s, B_W), BF16),
        compiler_params=_cparams(("parallel", "parallel", "arbitrary")),
        name="nbr_attn",
    )(qb, kb, kb, kb, proj3, proj3, proj3, bias_tab)


def _merge_kernel(ya_ref, yb_ref, ga_ref, gb_ref, x_ref, wpa_ref, wpb_ref, wout_ref, g2_ref, wr_ref, br_ref,
                  xmid_o, hu_o, logit_o):
    a = jnp.dot(ya_ref[...], wpa_ref[...], preferred_element_type=F32)
    bb = jnp.dot(yb_ref[...], wpb_ref[...], preferred_element_type=F32)
    m = jax.nn.sigmoid(ga_ref[...].astype(F32)) * a + jax.nn.sigmoid(gb_ref[...].astype(F32)) * bb
    xm = x_ref[...] + jnp.dot(m.astype(BF16), wout_ref[...], preferred_element_type=F32)
    xmid_o[...] = xm
    ms = jnp.mean(xm * xm, axis=-1, keepdims=True)
    h = xm * lax.rsqrt(ms + EPS) * g2_ref[...]
    h_hi = h.astype(BF16)
    h_lo = (h - h_hi.astype(F32)).astype(BF16)
    both = jnp.dot(h_hi, wr_ref[...], preferred_element_type=F32)
    cross = jnp.dot(h_lo, wr_ref[:, :LANES], preferred_element_type=F32)
    logit_o[...] = both[:, :LANES] + (both[:, LANES:] + cross) + br_ref[...]
    _pack_rows(h, hu_o)


def _merge(ya, yb, proj, x2, wpa, wpb, wout, g2, wr, br, off):
    n, d = x2.shape
    r = d // (2 * LANES)
    tm = _tile(n, 256, 8)
    const = lambda i: (0, 0)
    wspec = lambda shape: pl.BlockSpec(shape, const, pipeline_mode=pl.Buffered(1))
    return pl.pallas_call(
        _merge_kernel,
        grid=(n // tm,),
        in_specs=[
            pl.BlockSpec((tm, A_Q_W), lambda i: (i, 0)),
            pl.BlockSpec((tm, B_W), lambda i: (i, 0)),
            pl.BlockSpec((tm, d), lambda i, o=off["ga"] // d: (i, o)),
            pl.BlockSpec((tm, d), lambda i, o=off["gb"] // d: (i, o)),
            pl.BlockSpec((tm, d), lambda i: (i, 0)),
            wspec((A_Q_W, d)), wspec((B_W, d)), wspec((d, d)),
            pl.BlockSpec((1, d), const),
            wspec((d, 2 * LANES)),
            pl.BlockSpec((1, LANES), const),
        ],
        out_specs=[
            pl.BlockSpec((tm, d), lambda i: (i, 0)),
            pl.BlockSpec((tm * r, LANES), lambda i: (i, 0)),
            pl.BlockSpec((tm, LANES), lambda i: (i, 0)),
        ],
        out_shape=[
            jax.ShapeDtypeStruct((n, d), F32),
            jax.ShapeDtypeStruct((n * r, LANES), U32),
            jax.ShapeDtypeStruct((n, LANES), F32),
        ],
        compiler_params=_cparams(("parallel",)),
        name="merge_outproj",
    )(ya, yb, proj, proj, x2, wpa, wpb, wout, g2, wr, br)


def _router_kernel(logit_ref, idx_o, w_o, rank_o, cnt_o, carry_ref):
    tm = logit_ref.shape[0]

    @pl.when(pl.program_id(0) == 0)
    def _():
        carry_ref[...] = jnp.zeros(carry_ref.shape, F32)

    l = logit_ref[...]
    lane = lax.broadcasted_iota(jnp.int32, (tm, LANES), 1)
    lane_f = lane.astype(F32)
    vals, idxs, hots = [], [], []
    for _ in range(TOP_K):
        mx = jnp.max(l, axis=-1, keepdims=True)
        idx = jnp.min(jnp.where(l == mx, lane_f, float(LANES)), axis=-1, keepdims=True)
        hot = lane_f == idx
        vals.append(mx)
        idxs.append(idx)
        hots.append(hot)
        l = jnp.where(hot, -3e38, l)

    exps = [jnp.exp(v - vals[0]) for v in vals]
    tot = exps[0]
    for e in exps[1:]:
        tot = tot + e

    sel = jnp.zeros((tm, LANES), F32)
    for hot in hots:
        sel = sel + hot.astype(F32)
    row = lax.broadcasted_iota(jnp.int32, (tm, tm), 0)
    colm = lax.broadcasted_iota(jnp.int32, (tm, tm), 1)
    lower = (colm < row).astype(BF16)
    before = jnp.dot(lower, sel.astype(BF16), preferred_element_type=F32) + carry_ref[...]

    idx_out = jnp.zeros((tm, LANES), F32)
    w_out = jnp.zeros((tm, LANES), F32)
    rank_out = jnp.zeros((tm, LANES), F32)
    for k in range(TOP_K):
        here = lane == k
        rank_k = jnp.sum(jnp.where(hots[k], before, 0.0), axis=-1, keepdims=True)
        idx_out = jnp.where(here, idxs[k], idx_out)
        w_out = jnp.where(here, exps[k] / tot, w_out)
        rank_out = jnp.where(here, rank_k, rank_out)
    idx_o[...] = idx_out.astype(jnp.int32)
    w_o[...] = w_out
    rank_o[...] = rank_out.astype(jnp.int32)
    carry_ref[...] = carry_ref[...] + jnp.sum(sel, axis=0, keepdims=True)
    cnt_o[...] = carry_ref[...].astype(jnp.int32)


def _router(logits):
    n = logits.shape[0]
    tm = _tile(n, 512, 8)
    blk = pl.BlockSpec((tm, LANES), lambda i: (i, 0))
    return pl.pallas_call(
        _router_kernel,
        grid=(n // tm,),
        in_specs=[blk],
        out_specs=[blk, blk, blk, pl.BlockSpec((1, LANES), lambda i: (0, 0))],
        out_shape=[
            jax.ShapeDtypeStruct((n, LANES), jnp.int32),
            jax.ShapeDtypeStruct((n, LANES), F32),
            jax.ShapeDtypeStruct((n, LANES), jnp.int32),
            jax.ShapeDtypeStruct((1, LANES), jnp.int32),
        ],
        scratch_shapes=[pltpu.VMEM((1, LANES), F32)],
        compiler_params=_cparams(("arbitrary",)),
        name="router_topk",
    )(logits)


def _dispatch_kernel(dest_ref, h_ref, xs_in, xs_out, sem, *, r):
    del xs_in
    n = dest_ref.shape[0]

    def tile_copy(i):
        src = pl.multiple_of(lax.div(i, TOP_K) * r, r)
        dst = pl.multiple_of(dest_ref[i] * r, r)
        return pltpu.make_async_copy(h_ref.at[pl.ds(src, r)], xs_out.at[pl.ds(dst, r)], sem)

    def start(i, c):
        tile_copy(i).start()
        return c

    def wait(i, c):
        tile_copy(i).wait()
        return c

    lax.fori_loop(0, n, start, 0, unroll=DMA_UNROLL)
    lax.fori_loop(0, n, wait, 0, unroll=DMA_UNROLL)


def _dispatch(dest_flat, hu, n_slots, r):
    n = hu.shape[0] // r
    tb = _tile(n, 256, 8)
    xs0 = jnp.zeros((n_slots * r, LANES), U32)
    return pl.pallas_call(
        functools.partial(_dispatch_kernel, r=r),
        grid=(n // tb,),
        in_specs=[
            pl.BlockSpec((tb * TOP_K,), lambda i: (i,), memory_space=pltpu.SMEM),
            pl.BlockSpec((tb * r, LANES), lambda i: (i, 0)),
            pl.BlockSpec(memory_space=pl.ANY),
        ],
        out_specs=pl.BlockSpec(memory_space=pl.ANY),
        out_shape=jax.ShapeDtypeStruct((n_slots * r, LANES), U32),
        scratch_shapes=[pltpu.SemaphoreType.DMA],
        input_output_aliases={2: 0},
        compiler_params=_cparams(("arbitrary",)),
        name="moe_dispatch",
    )(dest_flat, hu, xs0)


def _used_block(b, nu):
    return jnp.minimum(b, nu[0] - 1)


def _moe_up_kernel(be_ref, nu_ref, xs_ref, wg_ref, wu_ref, bg_ref, bu_ref, o_ref, *, r):
    del be_ref

    @pl.when(pl.program_id(1) < nu_ref[0])
    def _():
        half = r * LANES
        words = [xs_ref[pl.ds(s, MOE_TM, stride=r), :] for s in range(r)]
        lo = jnp.concatenate([_unpack_lo(u).astype(BF16) for u in words], axis=1)
        hi = jnp.concatenate([_unpack_hi(u).astype(BF16) for u in words], axis=1)

        def proj(w_ref, b_ref):
            return (jnp.dot(lo, w_ref[0, :half, :], preferred_element_type=F32)
                    + jnp.dot(hi, w_ref[0, half:, :], preferred_element_type=F32) + b_ref[0])

        gate = jnp.minimum(proj(wg_ref, bg_ref), SWIGLU_LIMIT)
        up = jnp.clip(proj(wu_ref, bu_ref), -SWIGLU_LIMIT, SWIGLU_LIMIT)
        o_ref[...] = (gate * jax.nn.sigmoid(SWIGLU_ALPHA * gate) * (up + 1.0)).astype(o_ref.dtype)

    @pl.when(pl.program_id(1) >= nu_ref[0])
    def _():
        o_ref[...] = jnp.zeros(o_ref.shape, o_ref.dtype)


def _moe_up(block_e, n_used, xs, wg, wu, bg, bu, r):
    n_slots = xs.shape[0] // r
    e, d, de = wg.shape
    tn = _tile(de, 1024, LANES)
    nb = n_slots // MOE_TM
    grid_spec = pltpu.PrefetchScalarGridSpec(
        num_scalar_prefetch=2,
        grid=(de // tn, nb),
        in_specs=[
            pl.BlockSpec((MOE_TM * r, LANES), lambda j, b, be, nu: (_used_block(b, nu), 0)),
            pl.BlockSpec((1, d, tn), lambda j, b, be, nu: (be[_used_block(b, nu)], 0, j)),
            pl.BlockSpec((1, d, tn), lambda j, b, be, nu: (be[_used_block(b, nu)], 0, j)),
            pl.BlockSpec((1, 1, tn), lambda j, b, be, nu: (be[_used_block(b, nu)], 0, j)),
            pl.BlockSpec((1, 1, tn), lambda j, b, be, nu: (be[_used_block(b, nu)], 0, j)),
        ],
        out_specs=pl.BlockSpec((MOE_TM, tn), lambda j, b, be, nu: (b, j)),
    )
    return pl.pallas_call(
        functools.partial(_moe_up_kernel, r=r),
        grid_spec=grid_spec,
        out_shape=jax.ShapeDtypeStruct((n_slots, de), BF16),
        compiler_params=_cparams(("arbitrary", "arbitrary")),
        name="moe_up",
    )(block_e, n_used, xs, wg, wu, bg, bu)


def _moe_down_kernel(be_ref, nu_ref, act_ref, wd_ref, bd_ref, o_ref):
    del be_ref

    @pl.when(pl.program_id(0) < nu_ref[0])
    def _():
        y = jnp.dot(act_ref[...], wd_ref[0], preferred_element_type=F32) + bd_ref[0]
        _pack_rows(y, o_ref)

    @pl.when(pl.program_id(0) >= nu_ref[0])
    def _():
        o_ref[...] = jnp.zeros(o_ref.shape, o_ref.dtype)


def _moe_down(block_e, n_used, act, wd, bd):
    n_slots, de = act.shape
    d = wd.shape[2]
    r = d // (2 * LANES)
    nb = n_slots // MOE_TM
    grid_spec = pltpu.PrefetchScalarGridSpec(
        num_scalar_prefetch=2,
        grid=(nb,),
        in_specs=[
            pl.BlockSpec((MOE_TM, de), lambda b, be, nu: (_used_block(b, nu), 0)),
            pl.BlockSpec((1, de, d), lambda b, be, nu: (be[_used_block(b, nu)], 0, 0)),
            pl.BlockSpec((1, 1, d), lambda b, be, nu: (be[_used_block(b, nu)], 0, 0)),
        ],
        out_specs=pl.BlockSpec((MOE_TM * r, LANES), lambda b, be, nu: (b, 0)),
    )
    return pl.pallas_call(
        _moe_down_kernel,
        grid_spec=grid_spec,
        out_shape=jax.ShapeDtypeStruct((n_slots * r, LANES), U32),
        compiler_params=_cparams(("arbitrary",)),
        name="moe_down",
    )(block_e, n_used, act, wd, bd)


def _combine_kernel(dest_ref, x_ref, w_ref, y_hbm, o_ref, ybuf, sem, *, r):
    n = dest_ref.shape[0]
    tb, d = x_ref.shape
    half = d // 2

    def tile_copy(i):
        src = pl.multiple_of(dest_ref[i] * r, r)
        dst = pl.multiple_of(lax.div(i, TOP_K) * r, r)
        return pltpu.make_async_copy(y_hbm.at[pl.ds(src, r)], ybuf.at[lax.rem(i, TOP_K), pl.ds(dst, r)], sem)

    def start(i, c):
        tile_copy(i).start()
        return c

    def wait(i, c):
        tile_copy(i).wait()
        return c

    lax.fori_loop(0, n, start, 0, unroll=DMA_UNROLL)
    lax.fori_loop(0, n, wait, 0, unroll=DMA_UNROLL)

    weights = [w_ref[:, k:k + 1] for k in range(TOP_K)]
    for s in range(r):
        lo = x_ref[:, s * LANES:(s + 1) * LANES]
        hi = x_ref[:, half + s * LANES:half + (s + 1) * LANES]
        for k in range(TOP_K):
            yu = ybuf[k, pl.ds(s, tb, stride=r), :]
            lo = lo + weights[k] * _unpack_lo(yu)
            hi = hi + weights[k] * _unpack_hi(yu)
        o_ref[:, s * LANES:(s + 1) * LANES] = lo
        o_ref[:, half + s * LANES:half + (s + 1) * LANES] = hi


def _combine(dest_flat, xmid, w_pad, y, r):
    n, d = xmid.shape
    tb = _tile(n, 256, 8)
    return pl.pallas_call(
        functools.partial(_combine_kernel, r=r),
        grid=(n // tb,),
        in_specs=[
            pl.BlockSpec((tb * TOP_K,), lambda i: (i,), memory_space=pltpu.SMEM),
            pl.BlockSpec((tb, d), lambda i: (i, 0)),
            pl.BlockSpec((tb, LANES), lambda i: (i, 0)),
            pl.BlockSpec(memory_space=pl.ANY),
        ],
        out_specs=pl.BlockSpec((tb, d), lambda i: (i, 0)),
        out_shape=jax.ShapeDtypeStruct((n, d), F32),
        scratch_shapes=[pltpu.VMEM((TOP_K, tb * r, LANES), U32), pltpu.SemaphoreType.DMA],
        compiler_params=_cparams(("arbitrary",)),
        name="moe_combine",
    )(dest_flat, xmid, w_pad, y)


def _rope_tables(s):
    t = jnp.arange(s, dtype=jnp.int32)
    pos = jnp.stack([t // GRID_W, t % GRID_W], axis=-1).astype(F32)
    axis_dims = HEAD_DIM // 2
    inv_freq = ROPE_THETA ** (-jnp.arange(0, axis_dims, 2, dtype=F32) / axis_dims)
    ang = pos[:, :, None] * inv_freq[None, None, :]
    cos, sin = jnp.cos(ang), jnp.sin(ang)
    cos_t = jnp.concatenate([cos[:, 0], cos[:, 0], cos[:, 1], cos[:, 1]], axis=-1)
    sin_t = jnp.concatenate([-sin[:, 0], sin[:, 0], -sin[:, 1], sin[:, 1]], axis=-1)
    return cos_t, sin_t


def _layer(x2, b, s, attn_norm_g, w_in, a_q_norm_g, a_k_norm_g, b_q_norm_g, b_k_norm_g, na_rel_bias,
           w_branch_a, w_branch_b, w_out, ffn_norm_g, w_router, b_router,
           w_gate, b_gate, w_up, b_up, w_down, b_down):
    n, d = x2.shape
    e, _, de = w_gate.shape
    rows = s // GRID_W
    r = d // (2 * LANES)
    assert s % (NA_CHUNK_ROWS * GRID_W) == 0 and rows >= 2 * NA_CHUNK_ROWS and rows >= NA_ROWS
    assert d % (2 * LANES * SUBLANES) == 0 and e <= LANES and e >= TOP_K

    src = {}
    o = 0
    for name, width in (("qa", A_Q_W), ("ka", A_KV_W), ("va", A_KV_W), ("qb", B_W), ("kb", B_W), ("vb", B_W),
                        ("ga", d), ("gb", d)):
        src[name] = (o, width)
        o += width
    off = {}
    o = 0
    pieces = []
    for name in ("ga", "gb", "qa", "qb", "kb", "vb", "ka", "va"):
        off[name] = o
        pieces.append(w_in[:, src[name][0]:src[name][0] + src[name][1]])
        o += src[name][1]
    w_perm = jnp.concatenate(pieces, axis=1).astype(BF16)
    proj = _inproj(x2, attn_norm_g.reshape(1, d), w_perm)

    cos_t, sin_t = _rope_tables(s)
    hd = lambda g: g.reshape(1, HEAD_DIM)
    qa, ka, vat, qb, kb, qmax, kmax = _prep(proj, cos_t, sin_t, hd(a_q_norm_g), hd(a_k_norm_g),
                                            hd(b_q_norm_g), hd(b_k_norm_g), b, s, off)

    bound_max = jnp.sqrt(qmax[0, 0] * kmax[0, 0]) * BOUND_MARGIN
    flag = (bound_max <= BOUND_LIMIT).astype(jnp.int32).reshape(1)
    ya = _flash(qa.reshape(b, s, A_Q_W), ka.reshape(b, s, A_KV_W), vat, flag, kmax[0, :1])
    bias_tab = _na_bias_table(na_rel_bias, rows)
    yb = _na(qb.reshape(b, s, B_W), kb.reshape(b, s, B_W), proj.reshape(b, s, -1), off["vb"], bias_tab)

    wr32 = jnp.zeros((d, LANES), F32).at[:, :e].set(w_router)
    wr_hi = wr32.astype(BF16)
    wr = jnp.concatenate([wr_hi, (wr32 - wr_hi.astype(F32)).astype(BF16)], axis=1)
    br =jnp.full((1, LANES), NEG, F32).at[0, :e].set(b_router)
    xmid, hu, logits = _merge(ya.reshape(n, A_Q_W), yb.reshape(n, B_W), proj, x2,
                              w_branch_a.astype(BF16), w_branch_b.astype(BF16), w_out.astype(BF16),
                              ffn_norm_g.reshape(1, d), wr, br, off)

    idx_p, w_p, rank_p, cnt_p = _router(logits)

    counts = cnt_p[0, :e]
    padded = (counts + MOE_TM - 1) // MOE_TM * MOE_TM
    pad_end = jnp.cumsum(padded)
    pad_start = pad_end - padded
    nk = n * TOP_K
    n_blocks = -(-nk // MOE_TM) + e
    n_slots = n_blocks * MOE_TM
    top_idx = idx_p[:, :TOP_K]
    dest_flat = (pad_start[top_idx] + rank_p[:, :TOP_K]).reshape(nk).astype(jnp.int32)
    block_start = jnp.arange(n_blocks, dtype=jnp.int32) * MOE_TM
    block_e = jnp.minimum(jnp.sum(block_start[:, None] >= pad_end[None, :], axis=1), e - 1).astype(jnp.int32)
    n_used = (pad_end[-1] // MOE_TM).astype(jnp.int32).reshape(1)

    xs = _dispatch(dest_flat, hu, n_slots, r)
    act = _moe_up(block_e, n_used, xs, w_gate.astype(BF16), w_up.astype(BF16),
                  b_gate.reshape(e, 1, de), b_up.reshape(e, 1, de), r)
    y = _moe_down(block_e, n_used, act, w_down.astype(BF16), b_down.reshape(e, 1, d))
    return _combine(dest_flat, xmid, w_p, y, r)


def kernel(x, attn_norm_g, w_in, a_q_norm_g, a_k_norm_g, b_q_norm_g, b_k_norm_g, na_rel_bias, w_branch_a,
           w_branch_b, w_out, ffn_norm_g, w_router, b_router, w_gate, b_gate, w_up, b_up, w_down, b_down):
    b, s, d = x.shape
    x2 = x.reshape(b * s, d)
    for l in range(w_in.shape[0]):
        x2 = _layer(x2, b, s, attn_norm_g[l], w_in[l], a_q_norm_g[l], a_k_norm_g[l], b_q_norm_g[l],
                    b_k_norm_g[l], na_rel_bias[l], w_branch_a[l], w_branch_b[l], w_out[l], ffn_norm_g[l],
                    w_router[l], b_router[l], w_gate[l], b_gate[l], w_up[l], b_up[l], w_down[l], b_down[l])
    return x2.reshape(b, s, d)
```
